```python
import math
import jax
import jax.numpy as jnp
from jax import lax
import numpy as np

D_MODEL = 2048
BATCH = 1
SEQ = 8192
DEPTH = 4

N_MIXERS = 2
DEEPNORM_ALPHA = (2 * DEPTH) ** 0.25
DEEPNORM_BETA = (8 * DEPTH) ** -0.25
LN_EPS = 1e-5
NORM_EPS = 1e-6

CONV_WIDTH = 31

GDN_HEAD_K = 128
GDN_HEAD_V = 128
GDN_K_HEADS = D_MODEL // GDN_HEAD_K
GDN_V_HEADS = 2 * GDN_K_HEADS
GDN_QK_DIM = GDN_K_HEADS * GDN_HEAD_K
GDN_V_DIM = GDN_V_HEADS * GDN_HEAD_V
GDN_QKV_DIM = 2 * GDN_QK_DIM + GDN_V_DIM
GDN_IN_DIM = GDN_QKV_DIM + GDN_V_DIM + 2 * GDN_V_HEADS
GDN_CONV_WIDTH = 4
GDN_CHUNK = 64

N_EXPERTS = 16
N_GROUPS = 4
EXPERTS_PER_GROUP = N_EXPERTS // N_GROUPS
TOP_K = 2
D_EXPERT = 3 * D_MODEL // 4
MOE_BLOCK = 128

N_CONV_LAYERS = (DEPTH + 1) // 2
N_GDN_LAYERS = DEPTH // 2

kernel_name = 'hybrid_conformer_gdn_moe_deepnorm'


def layer_norm(x, g, b):
    xf = x.astype(jnp.float32)
    mu = jnp.mean(xf, axis=-1, keepdims=True)
    var = jnp.mean(jnp.square(xf - mu), axis=-1, keepdims=True)
    y = (xf - mu) * lax.rsqrt(var + LN_EPS) * g.astype(jnp.float32) + b.astype(jnp.float32)
    return y.astype(x.dtype)


def causal_depthwise_conv(x, w):
    width = w.shape[0]
    return lax.conv_general_dilated(
        x, w[:, None, :].astype(x.dtype), window_strides=(1,), padding=[(width - 1, 0)],
        dimension_numbers=('NWC', 'WIO', 'NWC'), feature_group_count=x.shape[-1])


def l2norm(x):
    return x * lax.rsqrt(jnp.sum(x * x, axis=-1, keepdims=True) + NORM_EPS)


def conformer_conv(x, w_pw1, b_pw1, w_dw, b_dw, ln_g, ln_b, w_pw2, b_pw2):
    h = x @ w_pw1 + b_pw1
    val, gate = jnp.split(h, 2, axis=-1)
    h = val * jax.nn.sigmoid(gate)
    h = causal_depthwise_conv(h, w_dw) + b_dw
    h = jax.nn.silu(layer_norm(h, ln_g, ln_b))
    return h @ w_pw2 + b_pw2


def chunk_gated_delta_rule(q, k, v, g, beta):
    b_, s_, h_, dk = q.shape
    dv = v.shape[-1]
    n = s_ // GDN_CHUNK
    q = q * (dk ** -0.5)

    def chunks(t):
        return t.reshape(b_, n, GDN_CHUNK, h_, -1).transpose(0, 3, 1, 2, 4)

    q, k, v = chunks(q), chunks(k), chunks(v)
    g = g.reshape(b_, n, GDN_CHUNK, h_).transpose(0, 3, 1, 2)
    beta = beta.reshape(b_, n, GDN_CHUNK, h_).transpose(0, 3, 1, 2)
    gc = jnp.cumsum(g, axis=-1)
    tril = jnp.tril(jnp.ones((GDN_CHUNK, GDN_CHUNK), dtype=bool))
    strict = jnp.tril(jnp.ones((GDN_CHUNK, GDN_CHUNK), dtype=bool), -1)
    decay = jnp.exp(jnp.where(tril, gc[..., :, None] - gc[..., None, :], -jnp.inf))

    kb = k * beta[..., None]
    m = jnp.where(strict, jnp.einsum('bhncd,bhnjd->bhncj', kb, k) * decay, 0.0)
    eye = jnp.eye(GDN_CHUNK, dtype=q.dtype)
    rhs = jnp.concatenate([v * beta[..., None], kb * jnp.exp(gc)[..., None]], axis=-1)
    sol = lax.linalg.triangular_solve(eye + m, rhs, left_side=True, lower=True, unit_diagonal=True)
    u, w = sol[..., :dv], sol[..., dv:]

    qk = jnp.where(tril, jnp.einsum('bhncd,bhnjd->bhncj', q, k) * decay, 0.0)
    q_dec = q * jnp.exp(gc)[..., None]
    k_dec = k * jnp.exp(gc[..., -1:] - gc)[..., None]
    g_last = jnp.exp(gc[..., -1])

    def step(state, inp):
        u_c, w_c, qk_c, q_c, k_c, gl = inp
        v_new = u_c - jnp.einsum('bhcd,bhde->bhce', w_c, state)
        o = jnp.einsum('bhcd,bhde->bhce', q_c, state) + jnp.einsum('bhcj,bhje->bhce', qk_c, v_new)
        state = state * gl[..., None, None] + jnp.einsum('bhcd,bhce->bhde', k_c, v_new)
        return state, o

    xs = tuple(jnp.moveaxis(t, 2, 0) for t in (u, w, qk, q_dec, k_dec, g_last))
    state0 = jnp.zeros((b_, h_, dk, dv), q.dtype)
    _, o = lax.scan(step, state0, xs)
    return o.transpose(1, 0, 3, 2, 4).reshape(b_, s_, h_, dv)


def gated_deltanet(x, w_in, w_conv, a_log, dt_bias, norm_w, w_out):
    b_, s_, _ = x.shape
    f32 = jnp.float32
    proj = x @ w_in
    qkv, z, bt, a = jnp.split(
        proj, [GDN_QKV_DIM, GDN_QKV_DIM + GDN_V_DIM, GDN_QKV_DIM + GDN_V_DIM + GDN_V_HEADS], axis=-1)
    qkv = jax.nn.silu(causal_depthwise_conv(qkv, w_conv))
    q, k, v = jnp.split(qkv, [GDN_QK_DIM, 2 * GDN_QK_DIM], axis=-1)
    rep = GDN_V_HEADS // GDN_K_HEADS
    q = jnp.repeat(l2norm(q.reshape(b_, s_, GDN_K_HEADS, GDN_HEAD_K).astype(f32)), rep, axis=2)
    k = jnp.repeat(l2norm(k.reshape(b_, s_, GDN_K_HEADS, GDN_HEAD_K).astype(f32)), rep, axis=2)
    v = v.reshape(b_, s_, GDN_V_HEADS, GDN_HEAD_V).astype(f32)
    beta = jax.nn.sigmoid(bt.astype(f32))
    g = -jnp.exp(a_log.astype(f32)) * jax.nn.softplus(a.astype(f32) + dt_bias.astype(f32))
    o = chunk_gated_delta_rule(q, k, v, g, beta)
    o = o * lax.rsqrt(jnp.mean(o * o, axis=-1, keepdims=True) + NORM_EPS) * norm_w.astype(f32)
    o = o * jax.nn.silu(z.reshape(b_, s_, GDN_V_HEADS, GDN_HEAD_V).astype(f32))
    return o.reshape(b_, s_, GDN_V_DIM).astype(x.dtype) @ w_out


def route(xf, router_w, router_b):
    logits = (xf @ router_w).astype(jnp.float32) + router_b.astype(jnp.float32)
    probs = jax.nn.softmax(logits, axis=-1)
    pg = probs.reshape(-1, N_GROUPS, EXPERTS_PER_GROUP)
    group_score = jnp.sum(lax.top_k(pg, TOP_K)[0], axis=-1)
    grp = jnp.argmax(group_score, axis=-1).astype(jnp.int32)
    in_grp = jnp.take_along_axis(pg, grp[:, None, None], axis=1)[:, 0]
    vals, local = lax.top_k(in_grp, TOP_K)
    idx = grp[:, None] * EXPERTS_PER_GROUP + local.astype(jnp.int32)
    gates = vals / jnp.sum(vals, axis=-1, keepdims=True)
    return idx, gates


def moe_ffn(x, router_w, router_b, w_gate, w_up, w_down):
    b_, s_, d = x.shape
    t = b_ * s_
    xf = x.reshape(t, d)
    idx, gates = route(xf, router_w, router_b)
    n_assign = t * TOP_K
    flat_e = idx.reshape(n_assign)
    order = jnp.argsort(flat_e)
    sorted_e = flat_e[order]
    tok = (order // TOP_K).astype(jnp.int32)
    gate_sorted = gates.reshape(n_assign)[order].astype(x.dtype)
    counts = jnp.zeros((N_EXPERTS,), jnp.int32).at[flat_e].add(1)
    padded = (counts + MOE_BLOCK - 1) // MOE_BLOCK * MOE_BLOCK
    start = jnp.cumsum(counts) - counts
    padded_end = jnp.cumsum(padded)
    padded_start = padded_end - padded
    dest = padded_start[sorted_e] + jnp.arange(n_assign, dtype=jnp.int32) - start[sorted_e]
    n_blocks = -(-n_assign // MOE_BLOCK) + N_EXPERTS
    n_rows = n_blocks * MOE_BLOCK
    row_tok = jnp.full((n_rows,), t, jnp.int32).at[dest].set(tok)
    x_pad = jnp.concatenate([xf, jnp.zeros((1, d), xf.dtype)], axis=0)
    xb = x_pad[row_tok].reshape(n_blocks, MOE_BLOCK, d)
    block_start = jnp.arange(n_blocks, dtype=jnp.int32) * MOE_BLOCK
    block_expert = jnp.minimum(
        jnp.sum((padded_end[None, :] <= block_start[:, None]).astype(jnp.int32), axis=1), N_EXPERTS - 1)

    def expert_block(args):
        xblk, e = args
        h = jax.nn.silu(xblk @ w_gate[e]) * (xblk @ w_up[e])
        return h @ w_down[e]

    yb = lax.map(expert_block, (xb, block_expert)).reshape(n_rows, d)
    out = jax.ops.segment_sum(yb[dest] * gate_sorted[:, None], tok, num_segments=t)
    return out.reshape(b_, s_, d)


def setup_inputs(seed: int = 0) -> dict:
    key = jax.random.key(seed)
    ks = jax.random.split(key, 24)
    f32 = jnp.float32
    D = D_MODEL
    NA, NB = N_CONV_LAYERS, N_GDN_LAYERS

    def nrm(k, shape, scale):
        return jax.random.normal(k, shape, f32) * scale

    dt = jnp.exp(jax.random.uniform(ks[12], (NB, GDN_V_HEADS), f32, math.log(1e-3), math.log(1e-1)))
    return {
        'x': nrm(ks[0], (BATCH, SEQ, D), 1.0),
        'conv_w_pw1': nrm(ks[1], (NA, D, 2 * D), D ** -0.5),
        'conv_b_pw1': nrm(ks[2], (NA, 2 * D), 0.01),
        'conv_w_dw': nrm(ks[3], (NA, CONV_WIDTH, D), CONV_WIDTH ** -0.5),
        'conv_b_dw': nrm(ks[4], (NA, D), 0.01),
        'conv_ln_g': 1.0 + nrm(ks[5], (NA, D), 0.01),
        'conv_ln_b': nrm(ks[6], (NA, D), 0.01),
        'conv_w_pw2': nrm(ks[7], (NA, D, D), D ** -0.5 * DEEPNORM_BETA),
        'conv_b_pw2': nrm(ks[8], (NA, D), 0.01),
        'gdn_w_in': nrm(ks[9], (NB, D, GDN_IN_DIM), D ** -0.5),
        'gdn_w_conv': nrm(ks[10], (NB, GDN_CONV_WIDTH, GDN_QKV_DIM), GDN_CONV_WIDTH ** -0.5),
        'gdn_a_log': jnp.log(jax.random.uniform(ks[11], (NB, GDN_V_HEADS), f32, 1.0, 16.0)),
        'gdn_dt_bias': dt + jnp.log(-jnp.expm1(-dt)),
        'gdn_norm_w': 1.0 + nrm(ks[13], (NB, GDN_HEAD_V), 0.01),
        'gdn_w_out': nrm(ks[14], (NB, GDN_V_DIM, D), GDN_V_DIM ** -0.5 * DEEPNORM_BETA),
        'router_w': nrm(ks[15], (D, N_EXPERTS), D ** -0.5),
        'router_b': nrm(ks[16], (N_EXPERTS,), 0.01),
        'moe_w_gate': nrm(ks[17], (DEPTH, N_EXPERTS, D, D_EXPERT), D ** -0.5),
        'moe_w_up': nrm(ks[18], (DEPTH, N_EXPERTS, D, D_EXPERT), D ** -0.5),
        'moe_w_down': nrm(ks[19], (DEPTH, N_EXPERTS, D_EXPERT, D), D_EXPERT ** -0.5 * DEEPNORM_BETA),
        'ln_mix_g': 1.0 + nrm(ks[20], (DEPTH, D), 0.01),
        'ln_mix_b': nrm(ks[21], (DEPTH, D), 0.01),
        'ln_ffn_g': 1.0 + nrm(ks[22], (DEPTH, D), 0.01),
        'ln_ffn_b': nrm(ks[23], (DEPTH, D), 0.01),
    }


def reference(x, conv_w_pw1, conv_b_pw1, conv_w_dw, conv_b_dw, conv_ln_g, conv_ln_b,
              conv_w_pw2, conv_b_pw2, gdn_w_in, gdn_w_conv, gdn_a_log, gdn_dt_bias,
              gdn_norm_w, gdn_w_out, router_w, router_b, moe_w_gate, moe_w_up, moe_w_down,
              ln_mix_g, ln_mix_b, ln_ffn_g, ln_ffn_b):
    for i in range(DEPTH):
        j = i // N_MIXERS
        if i % N_MIXERS == 0:
            h = conformer_conv(x, conv_w_pw1[j], conv_b_pw1[j], conv_w_dw[j], conv_b_dw[j],
                               conv_ln_g[j], conv_ln_b[j], conv_w_pw2[j], conv_b_pw2[j])
        else:
            h = gated_deltanet(x, gdn_w_in[j], gdn_w_conv[j], gdn_a_log[j], gdn_dt_bias[j],
                               gdn_norm_w[j], gdn_w_out[j])
        x = layer_norm(DEEPNORM_ALPHA * x + h, ln_mix_g[i], ln_mix_b[i])
        h = moe_ffn(x, router_w, router_b, moe_w_gate[i], moe_w_up[i], moe_w_down[i])
        x = layer_norm(DEEPNORM_ALPHA * x + h, ln_ffn_g[i], ln_ffn_b[i])
    return x
```

```python
import functools

import jax
import jax.numpy as jnp
from jax import lax
from jax.experimental import pallas as pl
from jax.experimental.pallas import tpu as pltpu

F32 = jnp.float32
BF16 = jnp.bfloat16

DEPTH = 4
DEEPNORM_ALPHA = (2 * DEPTH) ** 0.25
LN_EPS = 1e-5
NORM_EPS = 1e-6
HEAD_DIM = 128
GDN_CHUNK = 64
N_EXPERTS = 16
N_GROUPS = 4
EXPERTS_PER_GROUP = N_EXPERTS // N_GROUPS
TOP_K = 2
HALO_ROWS = 32
CARRY_ROWS = 8
MIB = 1024 * 1024


def _params(semantics, vmem_mib):
    return pltpu.CompilerParams(dimension_semantics=semantics, vmem_limit_bytes=vmem_mib * MIB)


def _sigmoid(x):
    return 1.0 / (1.0 + jnp.exp(-x))


def _silu(x):
    return x * _sigmoid(x)


def _layer_norm(y, g, b):
    mu = jnp.mean(y, axis=-1, keepdims=True)
    d = y - mu
    var = jnp.mean(d * d, axis=-1, keepdims=True)
    return d * lax.rsqrt(var + LN_EPS) * g + b


def _split_bf16(x):
    hi = x.astype(BF16)
    lo = (x - hi.astype(F32)).astype(BF16)
    return hi, lo


def _dot(a, b):
    return jnp.dot(a, b, preferred_element_type=F32)


def _dot_nt(a, b):
    return lax.dot_general(a, b, (((1,), (1,)), ((), ())), preferred_element_type=F32)


def _dot_tn(a, b):
    return lax.dot_general(a, b, (((0,), (0,)), ((), ())), preferred_element_type=F32)


def _dot3(x, w):
    xh, xl = _split_bf16(x)
    wh, wl = _split_bf16(w)
    return _dot(xh, wh) + _dot(xl, wh) + _dot(xh, wl)


def _pw1_glu_kernel(x_ref, wv_ref, wg_ref, bv_ref, bg_ref, o_ref, wv_s, wg_s):
    @pl.when(pl.program_id(1) == 0)
    def _():
        wv_s[...] = wv_ref[...].astype(BF16)
        wg_s[...] = wg_ref[...].astype(BF16)

    x = x_ref[...]
    val = _dot(x, wv_s[...]) + bv_ref[...]
    gate = _dot(x, wg_s[...]) + bg_ref[...]
    o_ref[...] = val * _sigmoid(gate)


def pw1_glu(xb, w, b, *, tm=1024, tn=512):
    m, k = xb.shape
    n = w.shape[1] // 2
    tm, tn = min(tm, m), min(tn, n)
    nj = n // tn
    b2 = b.reshape(1, 2 * n)
    return pl.pallas_call(
        _pw1_glu_kernel,
        grid=(nj, m // tm),
        in_specs=[
            pl.BlockSpec((tm, k), lambda j, i: (i, 0)),
            pl.BlockSpec((k, tn), lambda j, i: (0, j)),
            pl.BlockSpec((k, tn), lambda j, i: (0, j + nj)),
            pl.BlockSpec((1, tn), lambda j, i: (0, j)),
            pl.BlockSpec((1, tn), lambda j, i: (0, j + nj)),
        ],
        out_specs=pl.BlockSpec((tm, tn), lambda j, i: (i, j)),
        out_shape=jax.ShapeDtypeStruct((m, n), F32),
        scratch_shapes=[pltpu.VMEM((k, tn), BF16), pltpu.VMEM((k, tn), BF16)],
        compiler_params=_params(("arbitrary", "arbitrary"), 48),
        name="pw1_glu",
    )(xb, w, w, b2, b2)


def _dwconv_ln_silu_kernel(h_ref, halo_ref, w_ref, b_ref, g_ref, beta_ref, o_ref, buf, cbuf,
                           *, ts, width, cw, rb):
    i = pl.program_id(0)
    d = h_ref.shape[1]
    buf[0:HALO_ROWS, :] = jnp.where(i > 0, halo_ref[...], 0.0)
    buf[HALO_ROWS:HALO_ROWS + ts, :] = h_ref[...]

    def col_body(c, carry):
        cols = pl.ds(pl.multiple_of(c * cw, cw), cw)
        for r in range(ts // rb):
            acc = jnp.broadcast_to(b_ref[:, cols], (rb, cw))
            for j in range(width):
                start = r * rb + HALO_ROWS - (width - 1) + j
                acc = acc + w_ref[j:j + 1, cols] * buf[pl.ds(start, rb), cols]
            cbuf[pl.ds(r * rb, rb), cols] = acc
        return carry

    lax.fori_loop(0, d // cw, col_body, 0)
    y = _layer_norm(cbuf[...], g_ref[...], beta_ref[...])
    o_ref[...] = _silu(y).astype(o_ref.dtype)


def dwconv_ln_silu(h, w_dw, b_dw, ln_g, ln_b, *, ts=256, cw=256, rb=64):
    t, d = h.shape
    width = w_dw.shape[0]
    assert width - 1 <= HALO_ROWS
    ts, cw = min(ts, t), min(cw, d)
    halo_per_tile = ts // HALO_ROWS
    kern = functools.partial(_dwconv_ln_silu_kernel, ts=ts, width=width, cw=cw, rb=rb)
    row = lambda a: a.reshape(1, d)
    return pl.pallas_call(
        kern,
        grid=(t // ts,),
        in_specs=[
            pl.BlockSpec((ts, d), lambda i: (i, 0)),
            pl.BlockSpec((HALO_ROWS, d), lambda i: (jnp.maximum(i * halo_per_tile - 1, 0), 0)),
            pl.BlockSpec((width, d), lambda i: (0, 0)),
            pl.BlockSpec((1, d), lambda i: (0, 0)),
            pl.BlockSpec((1, d), lambda i: (0, 0)),
            pl.BlockSpec((1, d), lambda i: (0, 0)),
        ],
        out_specs=pl.BlockSpec((ts, d), lambda i: (i, 0)),
        out_shape=jax.ShapeDtypeStruct((t, d), BF16),
        scratch_shapes=[pltpu.VMEM((ts + HALO_ROWS, d), F32), pltpu.VMEM((ts, d), F32)],
        compiler_params=_params(("arbitrary",), 40),
        name="dwconv_ln_silu",
    )(h, h, w_dw, row(b_dw), row(ln_g), row(ln_b))


def _mm_res_ln_kernel(a_ref, w_ref, bias_ref, res_ref, g_ref, b_ref, o_ref, ob_ref, acc, *, nk):
    k = pl.program_id(1)
    part = _dot(a_ref[...], w_ref[...].astype(BF16))

    @pl.when(k == 0)
    def _():
        acc[...] = part

    @pl.when(k > 0)
    def _():
        acc[...] += part

    @pl.when(k == nk - 1)
    def _():
        y = DEEPNORM_ALPHA * res_ref[...] + (acc[...] + bias_ref[...])
        out = _layer_norm(y, g_ref[...], b_ref[...])
        o_ref[...] = out
        ob_ref[...] = out.astype(BF16)


def mm_res_ln(a, w, bias, res, ln_g, ln_b, *, tm=512, tk=1024):
    m, k = a.shape
    n = w.shape[1]
    tm, tk = min(tm, m), min(tk, k)
    nk = k // tk
    row = lambda v: v.reshape(1, n)
    return pl.pallas_call(
        functools.partial(_mm_res_ln_kernel, nk=nk),
        grid=(m // tm, nk),
        in_specs=[
            pl.BlockSpec((tm, tk), lambda i, kk: (i, kk)),
            pl.BlockSpec((tk, n), lambda i, kk: (kk, 0)),
            pl.BlockSpec((1, n), lambda i, kk: (0, 0)),
            pl.BlockSpec((tm, n), lambda i, kk: (i, 0)),
            pl.BlockSpec((1, n), lambda i, kk: (0, 0)),
            pl.BlockSpec((1, n), lambda i, kk: (0, 0)),
        ],
        out_specs=[pl.BlockSpec((tm, n), lambda i, kk: (i, 0)),
                   pl.BlockSpec((tm, n), lambda i, kk: (i, 0))],
        out_shape=[jax.ShapeDtypeStruct((m, n), F32), jax.ShapeDtypeStruct((m, n), BF16)],
        scratch_shapes=[pltpu.VMEM((tm, n), F32)],
        compiler_params=_params(("arbitrary", "arbitrary"), 56),
        name="mm_res_ln",
    )(a, w, row(bias), res, row(ln_g), row(ln_b))


def _gdn_in_kernel(x_ref, w_ref, wc_ref, o_ref, wbf, pbuf, *, tm, tn, nq, nqk, nconv, cwidth):
    j = pl.program_id(0)
    i = pl.program_id(1)

    @pl.when(i == 0)
    def _():
        wbf[...] = w_ref[...].astype(BF16)
        pbuf[0:CARRY_ROWS, :] = jnp.zeros((CARRY_ROWS, tn), F32)

    p = _dot(x_ref[...], wbf[...])

    @pl.when(j >= nconv)
    def _():
        o_ref[...] = _silu(p)

    @pl.when(j < nconv)
    def _():
        pbuf[CARRY_ROWS:CARRY_ROWS + tm, :] = p
        c = None
        for jj in range(cwidth):
            term = wc_ref[jj:jj + 1, :] * pbuf[pl.ds(CARRY_ROWS - (cwidth - 1) + jj, tm), :]
            c = term if c is None else c + term
        pbuf[0:CARRY_ROWS, :] = pbuf[tm:tm + CARRY_ROWS, :]
        s = _silu(c)

        @pl.when(j >= nqk)
        def _():
            o_ref[...] = s

        @pl.when(j < nqk)
        def _():
            scale = jnp.where(j < nq, HEAD_DIM ** -0.5, 1.0).astype(F32)
            for hh in range(tn // HEAD_DIM):
                cols = slice(hh * HEAD_DIM, (hh + 1) * HEAD_DIM)
                blk = s[:, cols]
                ss = jnp.sum(blk * blk, axis=-1, keepdims=True)
                o_ref[:, cols] = blk * lax.rsqrt(ss + NORM_EPS) * scale


def gdn_in_proj(xb, w_in, w_conv, *, tm=1024, tn=512):
    t, d = xb.shape
    qk_dim = d
    v_dim = 2 * d
    n_main = 2 * qk_dim + 2 * v_dim
    tm, tn = min(tm, t), min(tn, qk_dim)
    cwidth = w_conv.shape[0]
    nq = qk_dim // tn
    nconv = (2 * qk_dim + v_dim) // tn
    kern = functools.partial(_gdn_in_kernel, tm=tm, tn=tn, nq=nq, nqk=2 * nq, nconv=nconv,
                             cwidth=cwidth)
    return pl.pallas_call(
        kern,
        grid=(n_main // tn, t // tm),
        in_specs=[
            pl.BlockSpec((tm, d), lambda j, i: (i, 0)),
            pl.BlockSpec((d, tn), lambda j, i: (0, j)),
            pl.BlockSpec((cwidth, tn), lambda j, i: (0, jnp.minimum(j, nconv - 1))),
        ],
        out_specs=pl.BlockSpec((tm, tn), lambda j, i: (i, j)),
        out_shape=jax.ShapeDtypeStruct((t, n_main), F32),
        scratch_shapes=[pltpu.VMEM((d, tn), BF16), pltpu.VMEM((tm + CARRY_ROWS, tn), F32)],
        compiler_params=_params(("arbitrary", "arbitrary"), 48),
        name="gdn_in_proj",
    )(xb, w_in, w_conv)


def _gdn_gate_kernel(x_ref, wb_ref, wa_ref, alog_ref, dtb_ref, beta_ref, gc_ref, *, tm):
    x = x_ref[...]
    beta_ref[...] = _sigmoid(_dot3(x, wb_ref[...]))
    a = _dot3(x, wa_ref[...]) + dtb_ref[...]
    softplus = jnp.maximum(a, 0.0) + jnp.log(1.0 + jnp.exp(-jnp.abs(a)))
    g = -jnp.exp(alog_ref[...]) * softplus
    r = lax.broadcasted_iota(jnp.int32, (tm, tm), 0)
    c = lax.broadcasted_iota(jnp.int32, (tm, tm), 1)
    tri = jnp.where((r // GDN_CHUNK == c // GDN_CHUNK) & (c <= r), 1.0, 0.0).astype(BF16)
    g1 = g.astype(BF16)
    r1 = g - g1.astype(F32)
    g2 = r1.astype(BF16)
    g3 = (r1 - g2.astype(F32)).astype(BF16)
    gc_ref[...] = _dot(tri, g1) + _dot(tri, g2) + _dot(tri, g3)


def gdn_gates(x, w_beta, w_a, a_log, dt_bias, *, tm=512):
    t, d = x.shape
    h = w_beta.shape[1]
    tm = min(tm, t)
    row = lambda v: v.reshape(1, h)
    return pl.pallas_call(
        functools.partial(_gdn_gate_kernel, tm=tm),
        grid=(t // tm,),
        in_specs=[
            pl.BlockSpec((tm, d), lambda i: (i, 0)),
            pl.BlockSpec((d, h), lambda i: (0, 0)),
            pl.BlockSpec((d, h), lambda i: (0, 0)),
            pl.BlockSpec((1, h), lambda i: (0, 0)),
            pl.BlockSpec((1, h), lambda i: (0, 0)),
        ],
        out_specs=[pl.BlockSpec((tm, h), lambda i: (i, 0)), pl.BlockSpec((tm, h), lambda i: (i, 0))],
        out_shape=[jax.ShapeDtypeStruct((t, h), F32), jax.ShapeDtypeStruct((t, h), F32)],
        compiler_params=_params(("arbitrary",), 32),
        name="gdn_gates",
    )(x, w_beta, w_a, row(a_log), row(dt_bias))


def _gdn_chunk_kernel(q_ref, k_ref, v_ref, z_ref, col_ref, row_ref, nw_ref, o_ref, state, *, nc):
    c_ = GDN_CHUNK
    hd = HEAD_DIM

    @pl.when(pl.program_id(1) == 0)
    def _():
        state[...] = jnp.zeros(state.shape, F32)

    ii = lax.broadcasted_iota(jnp.int32, (c_, c_), 0)
    jj = lax.broadcasted_iota(jnp.int32, (c_, c_), 1)
    tril = ii >= jj
    strict = ii > jj
    nw = nw_ref[...]
    s_cur = [state[0], state[1]]

    for c in range(nc):
        rows = slice(c * c_, (c + 1) * c_)
        qc = q_ref[rows, :]
        kc = k_ref[rows, :]
        kcb = kc.astype(BF16)
        kbs, decays, egs, gls = [], [], [], []
        for hh in range(2):
            bcol = col_ref[rows, hh:hh + 1]
            gcol = col_ref[rows, 2 + hh:3 + hh]
            grow = row_ref[2 + hh, c:c + 1, :]
            dif = jnp.minimum(gcol - grow, 0.0)
            decays.append(jnp.where(tril, jnp.exp(dif), 0.0))
            kbs.append(kc * bcol)
            egs.append(jnp.exp(gcol))
            gls.append(gcol[c_ - 1:c_, :])
        lhs = jnp.concatenate([kbs[0], kbs[1], qc], axis=0).astype(BF16)
        aq = _dot_nt(lhs, kcb)
        for hh in range(2):
            decay = decays[hh]
            a = jnp.where(strict, aq[hh * c_:(hh + 1) * c_] * decay, 0.0)
            qk = (aq[2 * c_:3 * c_] * decay).astype(BF16)
            vc = v_ref[rows, hh * hd:(hh + 1) * hd]
            bcol = col_ref[rows, hh:hh + 1]
            rhs = jnp.concatenate([vc * bcol, kbs[hh] * egs[hh]], axis=1)
            pw = -a
            n_steps = c_.bit_length() - 1
            for s in range(n_steps):
                pwb = pw.astype(BF16)
                rhs = rhs + _dot(pwb, rhs.astype(BF16))
                if s + 1 < n_steps:
                    pw = _dot(pwb, pwb)
            u = rhs[:, :hd]
            w = rhs[:, hd:]
            q_dec = qc * egs[hh]
            k_dec = (kc * jnp.exp(gls[hh] - col_ref[rows, 2 + hh:3 + hh])).astype(BF16)
            sh = s_cur[hh]
            ws = _dot(jnp.concatenate([w, q_dec], axis=0).astype(BF16), sh.astype(BF16))
            v_new = u - ws[:c_]
            v_newb = v_new.astype(BF16)
            o = ws[c_:] + _dot(qk, v_newb)
            s_cur[hh] = sh * jnp.exp(gls[hh]) + _dot_tn(k_dec, v_newb)
            o = o * lax.rsqrt(jnp.mean(o * o, axis=-1, keepdims=True) + NORM_EPS) * nw
            o = o * z_ref[rows, hh * hd:(hh + 1) * hd]
            o_ref[rows, hh * hd:(hh + 1) * hd] = o.astype(o_ref.dtype)

    state[0] = s_cur[0]
    state[1] = s_cur[1]


def gdn_chunk(proj, beta, gc, norm_w, *, tb=512):
    t = proj.shape[0]
    nkh = beta.shape[1] // 2
    tb = min(tb, t)
    nc = tb // GDN_CHUNK
    n_chunks = t // GDN_CHUNK
    feat = jnp.stack([beta[:, 0::2], beta[:, 1::2], gc[:, 0::2], gc[:, 1::2]], axis=0)
    col = feat.transpose(2, 1, 0)
    row = feat.transpose(2, 0, 1).reshape(nkh, 4, n_chunks, GDN_CHUNK)
    v_blk = 2 * HEAD_DIM
    return pl.pallas_call(
        functools.partial(_gdn_chunk_kernel, nc=nc),
        grid=(nkh, t // tb),
        in_specs=[
            pl.BlockSpec((tb, HEAD_DIM), lambda h, n: (n, h)),
            pl.BlockSpec((tb, HEAD_DIM), lambda h, n: (n, nkh + h)),
            pl.BlockSpec((tb, v_blk), lambda h, n: (n, nkh + h)),
            pl.BlockSpec((tb, v_blk), lambda h, n: (n, 2 * nkh + h)),
            pl.BlockSpec((None, tb, 4), lambda h, n: (h, n, 0)),
            pl.BlockSpec((None, 4, nc, GDN_CHUNK), lambda h, n: (h, 0, n, 0)),
            pl.BlockSpec((1, HEAD_DIM), lambda h, n: (0, 0)),
        ],
        out_specs=pl.BlockSpec((tb, v_blk), lambda h, n: (n, h)),
        out_shape=jax.ShapeDtypeStruct((t, nkh * v_blk), BF16),
        scratch_shapes=[pltpu.VMEM((2, HEAD_DIM, HEAD_DIM), F32)],
        compiler_params=_params(("arbitrary", "arbitrary"), 32),
        name="gdn_chunk",
    )(proj, proj, proj, proj, col, row, norm_w.reshape(1, HEAD_DIM))


def _top2_of_4(a, b, c, d):
    hi1, lo1 = jnp.maximum(a, b), jnp.minimum(a, b)
    hi2, lo2 = jnp.maximum(c, d), jnp.minimum(c, d)
    return jnp.maximum(hi1, hi2), jnp.maximum(jnp.minimum(hi1, hi2), jnp.maximum(lo1, lo2))


def _first_argmax(vals):
    best, arg = vals[0], jnp.zeros(vals[0].shape, jnp.int32)
    for n in range(1, len(vals)):
        upd = vals[n] > best
        arg = jnp.where(upd, n, arg)
        best = jnp.where(upd, vals[n], best)
    return best, arg


def _router_kernel(x_ref, w_ref, b_ref, idx_ref, gate_ref, rank_ref, cnt_ref, run, *, tm):
    @pl.when(pl.program_id(0) == 0)
    def _():
        run[...] = jnp.zeros(run.shape, F32)

    xh, xl = _split_bf16(x_ref[...])
    wh, wl = _split_bf16(w_ref[...])
    logits = _dot_nt(wh, xh) + _dot_nt(wh, xl) + _dot_nt(wl, xh) + b_ref[...]
    e = jnp.exp(logits - jnp.max(logits, axis=0, keepdims=True))
    p = e / jnp.sum(e, axis=0, keepdims=True)
    rows = [p[r:r + 1, :] for r in range(N_EXPERTS)]
    scores = []
    for g in range(N_GROUPS):
        t1, t2 = _top2_of_4(*rows[g * EXPERTS_PER_GROUP:(g + 1) * EXPERTS_PER_GROUP])
        scores.append(t1 + t2)
    _, grp = _first_argmax(scores)
    vals = []
    for j in range(EXPERTS_PER_GROUP):
        v = rows[(N_GROUPS - 1) * EXPERTS_PER_GROUP + j]
        for g in range(N_GROUPS - 2, -1, -1):
            v = jnp.where(grp == g, rows[g * EXPERTS_PER_GROUP + j], v)
        vals.append(v)
    v1, i1 = _first_argmax(vals)
    v2, i2 = _first_argmax([jnp.where(i1 == j, -1.0, vals[j]) for j in range(EXPERTS_PER_GROUP)])
    e1 = grp * EXPERTS_PER_GROUP + i1
    e2 = grp * EXPERTS_PER_GROUP + i2
    den = v1 + v2
    idx_ref[0:1, :] = e1
    idx_ref[1:2, :] = e2
    gate_ref[0:1, :] = v1 / den
    gate_ref[1:2, :] = v2 / den

    eio = lax.broadcasted_iota(jnp.int32, (N_EXPERTS, tm), 0)
    hit1 = eio == e1
    hit2 = eio == e2
    onehot = jnp.where(hit1 | hit2, 1.0, 0.0)
    r = lax.broadcasted_iota(jnp.int32, (tm, tm), 0)
    c = lax.broadcasted_iota(jnp.int32, (tm, tm), 1)
    before = jnp.where(r < c, 1.0, 0.0).astype(BF16)
    prefix = _dot(onehot.astype(BF16), before) + run[:, 0:1]
    rank_ref[0:1, :] = jnp.sum(jnp.where(hit1, prefix, 0.0), axis=0, keepdims=True).astype(jnp.int32)
    rank_ref[1:2, :] = jnp.sum(jnp.where(hit2, prefix, 0.0), axis=0, keepdims=True).astype(jnp.int32)
    run[...] = run[...] + jnp.sum(onehot, axis=1, keepdims=True)
    cnt_ref[...] = run[...].astype(jnp.int32)


def moe_router(x, router_w, router_b, *, tm=512):
    t, d = x.shape
    tm = min(tm, t)
    out2 = lambda dt: jax.ShapeDtypeStruct((TOP_K, t), dt)
    blk2 = pl.BlockSpec((TOP_K, tm), lambda i: (0, i))
    return pl.pallas_call(
        functools.partial(_router_kernel, tm=tm),
        grid=(t // tm,),
        in_specs=[
            pl.BlockSpec((tm, d), lambda i: (i, 0)),
            pl.BlockSpec((N_EXPERTS, d), lambda i: (0, 0)),
            pl.BlockSpec((N_EXPERTS, 1), lambda i: (0, 0)),
        ],
        out_specs=[blk2, blk2, blk2, pl.BlockSpec((N_EXPERTS, 128), lambda i: (0, 0))],
        out_shape=[out2(jnp.int32), out2(F32), out2(jnp.int32),
                   jax.ShapeDtypeStruct((N_EXPERTS, 128), jnp.int32)],
        scratch_shapes=[pltpu.VMEM((N_EXPERTS, 128), F32)],
        compiler_params=_params(("arbitrary",), 32),
        name="moe_router",
    )(x, router_w.T, router_b.reshape(N_EXPERTS, 1))


def _dispatch_kernel(dest_ref, x_hbm, xs_in_hbm, xs_hbm, sem, *, td, t):
    del xs_in_hbm
    base = pl.program_id(0) * td

    def row_copy(src_row, dst_row):
        return pltpu.make_async_copy(x_hbm.at[pl.ds(src_row, 1)], xs_hbm.at[pl.ds(dst_row, 1)], sem)

    def issue(n, carry):
        for k in range(TOP_K):
            row_copy(base + n, dest_ref[k * t + base + n]).start()
        return carry

    def drain(n, carry):
        for k in range(TOP_K):
            row_copy(0, 0).wait()
        return carry

    lax.fori_loop(0, td, issue, 0)
    lax.fori_loop(0, td, drain, 0)


def moe_dispatch(x, dest_flat, n_rows, *, td=1024):
    t, d = x.shape
    td = min(td, t)
    any_spec = pl.BlockSpec(memory_space=pl.ANY)
    return pl.pallas_call(
        functools.partial(_dispatch_kernel, td=td, t=t),
        grid_spec=pltpu.PrefetchScalarGridSpec(
            num_scalar_prefetch=1,
            grid=(t // td,),
            in_specs=[any_spec, any_spec],
            out_specs=any_spec,
            scratch_shapes=[pltpu.SemaphoreType.DMA(())],
        ),
        out_shape=jax.ShapeDtypeStruct((n_rows, d), x.dtype),
        input_output_aliases={2: 0},
        compiler_params=_params(("arbitrary",), 32),
        name="moe_dispatch",
    )(dest_flat, x, jnp.zeros((n_rows, d), x.dtype))


def _moe_ffn_kernel(te_ref, nu_ref, x_ref, wg_ref, wu_ref, wd_ref, y_ref, *, tf):
    del te_ref

    @pl.when(pl.program_id(0) >= nu_ref[0])
    def _():
        y_ref[...] = jnp.zeros(y_ref.shape, F32)

    @pl.when(pl.program_id(0) < nu_ref[0])
    def _():
        x = x_ref[...].astype(BF16)
        f_dim = wg_ref.shape[1]
        acc = None
        for f in range(f_dim // tf):
            cols = slice(f * tf, (f + 1) * tf)
            h = _silu(_dot(x, wg_ref[:, cols])) * _dot(x, wu_ref[:, cols])
            part = _dot(h.astype(BF16), wd_ref[cols, :])
            acc = part if acc is None else acc + part
        y_ref[...] = acc


def moe_ffn(xs, tile_expert, n_used, wg, wu, wd, *, tm, tf=512):
    n_rows, d = xs.shape
    f_dim = wg.shape[2]
    tf = min(tf, f_dim)
    row_map = lambda i, te, nu: (jnp.minimum(i, nu[0] - 1), 0)
    return pl.pallas_call(
        functools.partial(_moe_ffn_kernel, tf=tf),
        grid_spec=pltpu.PrefetchScalarGridSpec(
            num_scalar_prefetch=2,
            grid=(n_rows // tm,),
            in_specs=[
                pl.BlockSpec((tm, d), row_map),
                pl.BlockSpec((None, d, f_dim), lambda i, te, nu: (te[i], 0, 0)),
                pl.BlockSpec((None, d, f_dim), lambda i, te, nu: (te[i], 0, 0)),
                pl.BlockSpec((None, f_dim, d), lambda i, te, nu: (te[i], 0, 0)),
            ],
            out_specs=pl.BlockSpec((tm, d), lambda i, te, nu: (i, 0)),
        ),
        out_shape=jax.ShapeDtypeStruct((n_rows, d), F32),
        compiler_params=_params(("arbitrary",), 58),
        name="moe_ffn",
    )(tile_expert, n_used, xs, wg, wu, wd)


def _combine_kernel(dest_ref, y_hbm, x_ref, gt_ref, g_ref, b_ref, o_ref, ob_ref, buf, sem,
                    *, tc, n_tiles, t):
    i = pl.program_id(0)

    def row_copy(src_row, slot, k, n):
        return pltpu.make_async_copy(y_hbm.at[pl.ds(src_row, 1)], buf.at[slot, k, pl.ds(n, 1)],
                                     sem.at[slot])

    def issue(tile, slot):
        def body(n, carry):
            for k in range(TOP_K):
                row_copy(dest_ref[k * t + tile * tc + n], slot, k, n).start()
            return carry
        lax.fori_loop(0, tc, body, 0)

    @pl.when(i == 0)
    def _():
        issue(0, 0)

    @pl.when(i + 1 < n_tiles)
    def _():
        issue(i + 1, (i + 1) % 2)

    slot = i % 2

    def drain(n, carry):
        for k in range(TOP_K):
            row_copy(0, slot, k, 0).wait()
        return carry

    lax.fori_loop(0, tc, drain, 0)
    gt = gt_ref[...]
    mix = gt[:, 0:1] * buf[slot, 0] + gt[:, 1:2] * buf[slot, 1]
    out = _layer_norm(DEEPNORM_ALPHA * x_ref[...] + mix, g_ref[...], b_ref[...])
    o_ref[...] = out
    ob_ref[...] = out.astype(BF16)


def moe_combine(y, dest_flat, x, gates_t, ln_g, ln_b, *, tc=256):
    t, d = x.shape
    tc = min(tc, t)
    n_tiles = t // tc
    row = lambda v: v.reshape(1, d)
    blk = lambda i, dest: (i, 0)
    fixed = lambda i, dest: (0, 0)
    return pl.pallas_call(
        functools.partial(_combine_kernel, tc=tc, n_tiles=n_tiles, t=t),
        grid_spec=pltpu.PrefetchScalarGridSpec(
            num_scalar_prefetch=1,
            grid=(n_tiles,),
            in_specs=[
                pl.BlockSpec(memory_space=pl.ANY),
                pl.BlockSpec((tc, d), blk),
                pl.BlockSpec((tc, TOP_K), blk),
                pl.BlockSpec((1, d), fixed),
                pl.BlockSpec((1, d), fixed),
            ],
            out_specs=[pl.BlockSpec((tc, d), blk), pl.BlockSpec((tc, d), blk)],
            scratch_shapes=[pltpu.VMEM((2, TOP_K, tc, d), F32), pltpu.SemaphoreType.DMA((2,))],
        ),
        out_shape=[jax.ShapeDtypeStruct((t, d), F32), jax.ShapeDtypeStruct((t, d), BF16)],
        compiler_params=_params(("arbitrary",), 40),
        name="moe_combine",
    )(dest_flat, y, x, gates_t, row(ln_g), row(ln_b))


def moe_layer(x, router_w, router_b, wg, wu, wd, ln_g, ln_b, *, tm=256):
    t, d = x.shape
    idx, gates, rank, cnt = moe_router(x, router_w, router_b)
    counts = cnt[:, 0]
    padded = (counts + tm - 1) // tm * tm
    pend = jnp.cumsum(padded)
    pstart = pend - padded
    dest = (pstart[idx] + rank).reshape(TOP_K * t).astype(jnp.int32)
    n_tiles = (t * TOP_K) // tm + N_EXPERTS
    n_used = (pend[-1] // tm).astype(jnp.int32)
    tile_start = jnp.arange(n_tiles, dtype=jnp.int32) * tm
    te = jnp.minimum(jnp.sum((pend[None, :] <= tile_start[:, None]).astype(jnp.int32), axis=1),
                     N_EXPERTS - 1)
    te = jnp.where(jnp.arange(n_tiles) < n_used, te, te[n_used - 1]).astype(jnp.int32)

    xs = moe_dispatch(x, dest, n_tiles * tm)
    y = moe_ffn(xs, te, n_used.reshape(1), wg, wu, wd, tm=tm)
    return moe_combine(y, dest, x, gates.T, ln_g, ln_b)


def kernel(x, conv_w_pw1, conv_b_pw1, conv_w_dw, conv_b_dw, conv_ln_g, conv_ln_b, conv_w_pw2,
           conv_b_pw2, gdn_w_in, gdn_w_conv, gdn_a_log, gdn_dt_bias, gdn_norm_w, gdn_w_out,
           router_w, router_b, moe_w_gate, moe_w_up, moe_w_down, ln_mix_g, ln_mix_b, ln_ffn_g,
           ln_ffn_b):
    b_, s_, d = x.shape
    t = b_ * s_
    xf = x.reshape(t, d)
    xb = xf.astype(BF16)
    wg_b = moe_w_gate.astype(BF16)
    wu_b = moe_w_up.astype(BF16)
    wd_b = moe_w_down.astype(BF16)
    n_main = 6 * d
    n_vh = gdn_a_log.shape[1]
    for i in range(DEPTH):
        j = i // 2
        if i % 2 == 0:
            h = pw1_glu(xb, conv_w_pw1[j], conv_b_pw1[j])
            hb = dwconv_ln_silu(h, conv_w_dw[j], conv_b_dw[j], conv_ln_g[j], conv_ln_b[j])
            xf, xb = mm_res_ln(hb, conv_w_pw2[j], conv_b_pw2[j], xf, ln_mix_g[i], ln_mix_b[i])
        else:
            proj = gdn_in_proj(xb, gdn_w_in[j], gdn_w_conv[j])
            beta, gc = gdn_gates(xf, gdn_w_in[j][:, n_main:n_main + n_vh],
                                 gdn_w_in[j][:, n_main + n_vh:], gdn_a_log[j], gdn_dt_bias[j])
            ob = gdn_chunk(proj, beta, gc, gdn_norm_w[j])
            xf, xb = mm_res_ln(ob, gdn_w_out[j], jnp.zeros((d,), F32), xf, ln_mix_g[i], ln_mix_b[i])
        xf, xb = moe_layer(xf, router_w, router_b, wg_b[i], wu_b[i], wd_b[i], ln_ffn_g[i], ln_ffn_b[i])
    return xf.reshape(b_, s_, d)
```

```python
import functools

import jax
import jax.numpy as jnp
from jax import lax
from jax.experimental import pallas as pl
from jax.experimental.pallas import tpu as pltpu

F32 = jnp.float32
BF16 = jnp.bfloat16

DEPTH = 4
DEEPNORM_ALPHA = (2 * DEPTH) ** 0.25
LN_EPS = 1e-5
NORM_EPS = 1e-6
HEAD_DIM = 128
GDN_CHUNK = 64
N_EXPERTS = 16
N_GROUPS = 4
EXPERTS_PER_GROUP = N_EXPERTS // N_GROUPS
TOP_K = 2
HALO_ROWS = 32
CARRY_ROWS = 8
GDN_KEY_HEADS_PER_STEP = 4
MIB = 1024 * 1024


def _params(semantics, vmem_mib):
    return pltpu.CompilerParams(dimension_semantics=semantics, vmem_limit_bytes=vmem_mib * MIB)


def _sigmoid(x):
    return 1.0 / (1.0 + jnp.exp(-x))


def _silu(x):
    return x * _sigmoid(x)


def _layer_norm(y, g, b):
    mu = jnp.mean(y, axis=-1, keepdims=True)
    d = y - mu
    var = jnp.mean(d * d, axis=-1, keepdims=True)
    return d * lax.rsqrt(var + LN_EPS) * g + b


def _split_bf16(x):
    hi = x.astype(BF16)
    lo = (x - hi.astype(F32)).astype(BF16)
    return hi, lo


def _dot(a, b):
    return jnp.dot(a, b, preferred_element_type=F32)


def _dot_nt(a, b):
    return lax.dot_general(a, b, (((1,), (1,)), ((), ())), preferred_element_type=F32)


def _dot_tn(a, b):
    return lax.dot_general(a, b, (((0,), (0,)), ((), ())), preferred_element_type=F32)


def _dot3(x, w):
    xh, xl = _split_bf16(x)
    wh, wl = _split_bf16(w)
    return _dot(xh, wh) + _dot(xl, wh) + _dot(xh, wl)


def _pw1_glu_kernel(x_ref, wv_ref, wg_ref, bv_ref, bg_ref, o_ref, wv_s, wg_s):
    @pl.when(pl.program_id(1) == 0)
    def _():
        wv_s[...] = wv_ref[...].astype(BF16)
        wg_s[...] = wg_ref[...].astype(BF16)

    x = x_ref[...]
    val = _dot(x, wv_s[...]) + bv_ref[...]
    gate = _dot(x, wg_s[...]) + bg_ref[...]
    o_ref[...] = val * _sigmoid(gate)


def pw1_glu(xb, w_all, b, layer, *, tm=1024, tn=512):
    m, k = xb.shape
    n = w_all.shape[2] // 2
    tm, tn = min(tm, m), min(tn, n)
    nj = n // tn
    b2 = b.reshape(1, 2 * n)
    w = w_all
    return pl.pallas_call(
        _pw1_glu_kernel,
        grid=(nj, m // tm),
        in_specs=[
            pl.BlockSpec((tm, k), lambda j, i: (i, 0)),
            pl.BlockSpec((None, k, tn), lambda j, i: (layer, 0, j)),
            pl.BlockSpec((None, k, tn), lambda j, i: (layer, 0, j + nj)),
            pl.BlockSpec((1, tn), lambda j, i: (0, j)),
            pl.BlockSpec((1, tn), lambda j, i: (0, j + nj)),
        ],
        out_specs=pl.BlockSpec((tm, tn), lambda j, i: (i, j)),
        out_shape=jax.ShapeDtypeStruct((m, n), F32),
        scratch_shapes=[pltpu.VMEM((k, tn), BF16), pltpu.VMEM((k, tn), BF16)],
        compiler_params=_params(("arbitrary", "arbitrary"), 48),
        name="pw1_glu",
    )(xb, w, w, b2, b2)


def _dwconv_ln_silu_kernel(h_ref, halo_ref, w_ref, b_ref, g_ref, beta_ref, o_ref, buf, cbuf,
                           *, ts, width, cw, rb):
    i = pl.program_id(0)
    d = h_ref.shape[1]
    buf[0:HALO_ROWS, :] = jnp.where(i > 0, halo_ref[...], 0.0)
    buf[HALO_ROWS:HALO_ROWS + ts, :] = h_ref[...]

    def col_body(c, carry):
        cols = pl.ds(pl.multiple_of(c * cw, cw), cw)
        for r in range(ts // rb):
            acc = jnp.broadcast_to(b_ref[:, cols], (rb, cw))
            for j in range(width):
                start = r * rb + HALO_ROWS - (width - 1) + j
                acc = acc + w_ref[j:j + 1, cols] * buf[pl.ds(start, rb), cols]
            cbuf[pl.ds(r * rb, rb), cols] = acc
        return carry

    lax.fori_loop(0, d // cw, col_body, 0)
    y = _layer_norm(cbuf[...], g_ref[...], beta_ref[...])
    o_ref[...] = _silu(y).astype(o_ref.dtype)


def dwconv_ln_silu(h, w_dw, b_dw, ln_g, ln_b, *, ts=256, cw=256, rb=64):
    t, d = h.shape
    width = w_dw.shape[0]
    assert width - 1 <= HALO_ROWS
    ts, cw = min(ts, t), min(cw, d)
    halo_per_tile = ts // HALO_ROWS
    kern = functools.partial(_dwconv_ln_silu_kernel, ts=ts, width=width, cw=cw, rb=rb)
    row = lambda a: a.reshape(1, d)
    return pl.pallas_call(
        kern,
        grid=(t // ts,),
        in_specs=[
            pl.BlockSpec((ts, d), lambda i: (i, 0)),
            pl.BlockSpec((HALO_ROWS, d), lambda i: (jnp.maximum(i * halo_per_tile - 1, 0), 0)),
            pl.BlockSpec((width, d), lambda i: (0, 0)),
            pl.BlockSpec((1, d), lambda i: (0, 0)),
            pl.BlockSpec((1, d), lambda i: (0, 0)),
            pl.BlockSpec((1, d), lambda i: (0, 0)),
        ],
        out_specs=pl.BlockSpec((ts, d), lambda i: (i, 0)),
        out_shape=jax.ShapeDtypeStruct((t, d), BF16),
        scratch_shapes=[pltpu.VMEM((ts + HALO_ROWS, d), F32), pltpu.VMEM((ts, d), F32)],
        compiler_params=_params(("arbitrary",), 40),
        name="dwconv_ln_silu",
    )(h, h, w_dw, row(b_dw), row(ln_g), row(ln_b))


def _mm_res_ln_kernel(a_ref, w_ref, bias_ref, res_ref, g_ref, b_ref, o_ref, ob_ref, acc, *, nk):
    k = pl.program_id(1)
    part = _dot(a_ref[...], w_ref[...].astype(BF16))

    @pl.when(k == 0)
    def _():
        acc[...] = part

    @pl.when(k > 0)
    def _():
        acc[...] += part

    @pl.when(k == nk - 1)
    def _():
        y = DEEPNORM_ALPHA * res_ref[...] + (acc[...] + bias_ref[...])
        out = _layer_norm(y, g_ref[...], b_ref[...])
        o_ref[...] = out
        ob_ref[...] = out.astype(BF16)


def mm_res_ln(a, w_all, layer, bias, res, ln_g, ln_b, *, tm=512, tk=1024):
    m, k = a.shape
    n = w_all.shape[2]
    tm, tk = min(tm, m), min(tk, k)
    nk = k // tk
    row = lambda v: v.reshape(1, n)
    return pl.pallas_call(
        functools.partial(_mm_res_ln_kernel, nk=nk),
        grid=(m // tm, nk),
        in_specs=[
            pl.BlockSpec((tm, tk), lambda i, kk: (i, kk)),
            pl.BlockSpec((None, tk, n), lambda i, kk: (layer, kk, 0)),
            pl.BlockSpec((1, n), lambda i, kk: (0, 0)),
            pl.BlockSpec((tm, n), lambda i, kk: (i, 0)),
            pl.BlockSpec((1, n), lambda i, kk: (0, 0)),
            pl.BlockSpec((1, n), lambda i, kk: (0, 0)),
        ],
        out_specs=[pl.BlockSpec((tm, n), lambda i, kk: (i, 0)),
                   pl.BlockSpec((tm, n), lambda i, kk: (i, 0))],
        out_shape=[jax.ShapeDtypeStruct((m, n), F32), jax.ShapeDtypeStruct((m, n), BF16)],
        scratch_shapes=[pltpu.VMEM((tm, n), F32)],
        compiler_params=_params(("arbitrary", "arbitrary"), 56),
        name="mm_res_ln",
    )(a, w_all, row(bias), res, row(ln_g), row(ln_b))


def _gdn_in_kernel(x_ref, w_ref, wc_ref, o_ref, wbf, pbuf, *, tm, tn, nq, nqk, nconv, cwidth):
    j = pl.program_id(0)
    i = pl.program_id(1)
    heads = [slice(hh * HEAD_DIM, (hh + 1) * HEAD_DIM) for hh in range(tn // HEAD_DIM)]

    @pl.when(i == 0)
    def _():
        wbf[...] = w_ref[...].astype(BF16)
        pbuf[0:CARRY_ROWS, :] = jnp.zeros((CARRY_ROWS, tn), F32)

    p = _dot(x_ref[...], wbf[...])

    @pl.when(j >= nconv)
    def _():
        s = _silu(p)
        for hh, cols in enumerate(heads):
            o_ref[hh] = s[:, cols]

    @pl.when(j < nconv)
    def _():
        pbuf[CARRY_ROWS:CARRY_ROWS + tm, :] = p
        c = None
        for jj in range(cwidth):
            term = wc_ref[jj:jj + 1, :] * pbuf[pl.ds(CARRY_ROWS - (cwidth - 1) + jj, tm), :]
            c = term if c is None else c + term
        pbuf[0:CARRY_ROWS, :] = pbuf[tm:tm + CARRY_ROWS, :]
        s = _silu(c)

        @pl.when(j >= nqk)
        def _():
            for hh, cols in enumerate(heads):
                o_ref[hh] = s[:, cols]

        @pl.when(j < nqk)
        def _():
            scale = jnp.where(j < nq, HEAD_DIM ** -0.5, 1.0).astype(F32)
            for hh, cols in enumerate(heads):
                blk = s[:, cols]
                ss = jnp.sum(blk * blk, axis=-1, keepdims=True)
                o_ref[hh] = blk * lax.rsqrt(ss + NORM_EPS) * scale


def gdn_in_proj(xb, w_all, layer, w_conv, *, tm=1024, tn=512):
    t, d = xb.shape
    qk_dim = d
    v_dim = 2 * d
    n_main = 2 * qk_dim + 2 * v_dim
    tm, tn = min(tm, t), min(tn, qk_dim)
    cwidth = w_conv.shape[0]
    nq = qk_dim // tn
    nconv = (2 * qk_dim + v_dim) // tn
    hpt = tn // HEAD_DIM
    kern = functools.partial(_gdn_in_kernel, tm=tm, tn=tn, nq=nq, nqk=2 * nq, nconv=nconv,
                             cwidth=cwidth)
    return pl.pallas_call(
        kern,
        grid=(n_main // tn, t // tm),
        in_specs=[
            pl.BlockSpec((tm, d), lambda j, i: (i, 0)),
            pl.BlockSpec((None, d, tn), lambda j, i: (layer, 0, j)),
            pl.BlockSpec((cwidth, tn), lambda j, i: (0, jnp.minimum(j, nconv - 1))),
        ],
        out_specs=pl.BlockSpec((hpt, tm, HEAD_DIM), lambda j, i: (j, i, 0)),
        out_shape=jax.ShapeDtypeStruct((n_main // HEAD_DIM, t, HEAD_DIM), F32),
        scratch_shapes=[pltpu.VMEM((d, tn), BF16), pltpu.VMEM((tm + CARRY_ROWS, tn), F32)],
        compiler_params=_params(("arbitrary", "arbitrary"), 48),
        name="gdn_in_proj",
    )(xb, w_all, w_conv)


def _gdn_gate_kernel(x_ref, wb_ref, wa_ref, alog_ref, dtb_ref, col_ref, gct_ref, *, tm, hg):
    x = x_ref[...]
    beta = _sigmoid(_dot3(x, wb_ref[...]))
    a = _dot3(x, wa_ref[...]) + dtb_ref[...]
    softplus = jnp.maximum(a, 0.0) + jnp.log(1.0 + jnp.exp(-jnp.abs(a)))
    g = -jnp.exp(alog_ref[...]) * softplus
    r = lax.broadcasted_iota(jnp.int32, (tm, tm), 0)
    c = lax.broadcasted_iota(jnp.int32, (tm, tm), 1)
    same = r // GDN_CHUNK == c // GDN_CHUNK
    lower = jnp.where(same & (c <= r), 1.0, 0.0).astype(BF16)
    upper = jnp.where(same & (r <= c), 1.0, 0.0).astype(BF16)
    g1 = g.astype(BF16)
    r1 = g - g1.astype(F32)
    g2 = r1.astype(BF16)
    g3 = (r1 - g2.astype(F32)).astype(BF16)
    gc = _dot(lower, g1) + _dot(lower, g2) + _dot(lower, g3)
    gct_ref[...] = _dot_tn(g1, upper) + _dot_tn(g2, upper) + _dot_tn(g3, upper)
    for grp in range(col_ref.shape[0]):
        cols = slice(grp * hg, (grp + 1) * hg)
        col_ref[grp] = jnp.concatenate([beta[:, cols], gc[:, cols]], axis=1)


def gdn_gates(x, w_beta, w_a, a_log, dt_bias, *, hg, tm=512):
    t, d = x.shape
    h = w_beta.shape[1]
    tm = min(tm, t)
    row = lambda v: v.reshape(1, h)
    return pl.pallas_call(
        functools.partial(_gdn_gate_kernel, tm=tm, hg=hg),
        grid=(t // tm,),
        in_specs=[
            pl.BlockSpec((tm, d), lambda i: (i, 0)),
            pl.BlockSpec((d, h), lambda i: (0, 0)),
            pl.BlockSpec((d, h), lambda i: (0, 0)),
            pl.BlockSpec((1, h), lambda i: (0, 0)),
            pl.BlockSpec((1, h), lambda i: (0, 0)),
        ],
        out_specs=[pl.BlockSpec((h // hg, tm, 2 * hg), lambda i: (0, i, 0)),
                   pl.BlockSpec((h, tm), lambda i: (0, i))],
        out_shape=[jax.ShapeDtypeStruct((h // hg, t, 2 * hg), F32),
                   jax.ShapeDtypeStruct((h, t), F32)],
        compiler_params=_params(("arbitrary",), 32),
        name="gdn_gates",
    )(x, w_beta, w_a, row(a_log), row(dt_bias))


def _gdn_chunk_kernel(q_ref, k_ref, v_ref, z_ref, col_ref, row_ref, nw_ref, o_ref, state,
                      *, nc, kg):
    c_ = GDN_CHUNK
    hd = HEAD_DIM
    vh = 2 * kg

    @pl.when(pl.program_id(1) == 0)
    def _():
        state[...] = jnp.zeros(state.shape, F32)

    ii = lax.broadcasted_iota(jnp.int32, (c_, c_), 0)
    jj = lax.broadcasted_iota(jnp.int32, (c_, c_), 1)
    tril = ii >= jj
    strict = ii > jj

    q = q_ref[...].reshape(kg * nc, c_, hd)
    k = k_ref[...].reshape(kg * nc, c_, hd)
    kq = lax.dot_general(jnp.concatenate([k, q], axis=1).astype(BF16), k.astype(BF16),
                         (((2,), (2,)), ((0,), (0,))), preferred_element_type=F32)
    col = col_ref[...]
    neg_a, rhs, qk, k_dec, q_dec, g_last = [], [], [], [], [], []
    for h in range(vh):
        own = slice((h // 2) * nc, (h // 2 + 1) * nc)
        bcol = col[:, h:h + 1].reshape(nc, c_, 1)
        gcol = col[:, vh + h:vh + h + 1].reshape(nc, c_, 1)
        grow = row_ref[h]
        decay = jnp.where(tril, jnp.exp(jnp.minimum(gcol - grow, 0.0)), 0.0)
        neg_a.append(jnp.where(strict, -(bcol * kq[own, :c_] * decay), 0.0))
        qk.append((kq[own, c_:] * decay).astype(BF16))
        eg = jnp.exp(gcol)
        glast = gcol[:, c_ - 1:c_, :]
        rhs.append(jnp.concatenate([v_ref[h].reshape(nc, c_, hd) * bcol, k[own] * (bcol * eg)],
                                   axis=2))
        k_dec.append((k[own] * jnp.exp(glast - gcol)).astype(BF16))
        q_dec.append(q[own] * eg)
        g_last.append(jnp.exp(glast))

    def chunk_major(per_head):
        stacked = jnp.stack(per_head, axis=1)
        return stacked.reshape((nc * vh,) + stacked.shape[2:])

    pw = chunk_major(neg_a)
    sol = chunk_major(rhs)
    n_steps = c_.bit_length() - 1
    bmm = lambda a, b: jnp.einsum("bij,bjk->bik", a, b, preferred_element_type=F32)
    for s in range(n_steps):
        pwb = pw.astype(BF16)
        sol = sol + bmm(pwb, sol.astype(BF16))
        if s + 1 < n_steps:
            pw = bmm(pwb, pwb)
    u = sol[:, :, :hd]
    wq = jnp.concatenate([sol[:, :, hd:], chunk_major(q_dec)], axis=1).astype(BF16)
    qk = chunk_major(qk)
    k_dec = chunk_major(k_dec)
    g_last = chunk_major(g_last)

    nw = nw_ref[...]
    s_cur = [state[h] for h in range(vh)]
    for c in range(nc):
        rows = slice(c * c_, (c + 1) * c_)
        ws = [_dot(wq[c * vh + h], s_cur[h].astype(BF16)) for h in range(vh)]
        v_new = [(u[c * vh + h] - ws[h][:c_]).astype(BF16) for h in range(vh)]
        o = [ws[h][c_:] + _dot(qk[c * vh + h], v_new[h]) for h in range(vh)]
        s_cur = [s_cur[h] * g_last[c * vh + h] + _dot_tn(k_dec[c * vh + h], v_new[h])
                 for h in range(vh)]
        for h in range(vh):
            on = o[h] * lax.rsqrt(jnp.mean(o[h] * o[h], axis=-1, keepdims=True) + NORM_EPS) * nw
            o_ref[rows, h * hd:(h + 1) * hd] = (on * z_ref[h, rows, :]).astype(o_ref.dtype)
    for h in range(vh):
        state[h] = s_cur[h]


def gdn_chunk(proj, col, gct, norm_w, *, kg, tb=512):
    t = proj.shape[1]
    nkh = proj.shape[0] // 6
    vh = 2 * kg
    tb = min(tb, t)
    nc = tb // GDN_CHUNK
    n_chunks = t // GDN_CHUNK
    row = gct.reshape(nkh // kg, vh, n_chunks, 1, GDN_CHUNK)
    return pl.pallas_call(
        functools.partial(_gdn_chunk_kernel, nc=nc, kg=kg),
        grid=(nkh // kg, t // tb),
        in_specs=[
            pl.BlockSpec((kg, tb, HEAD_DIM), lambda g, n: (g, n, 0)),
            pl.BlockSpec((kg, tb, HEAD_DIM), lambda g, n: (nkh // kg + g, n, 0)),
            pl.BlockSpec((vh, tb, HEAD_DIM), lambda g, n: (2 * nkh // vh + g, n, 0)),
            pl.BlockSpec((vh, tb, HEAD_DIM), lambda g, n: (4 * nkh // vh + g, n, 0)),
            pl.BlockSpec((None, tb, 2 * vh), lambda g, n: (g, n, 0)),
            pl.BlockSpec((None, vh, nc, 1, GDN_CHUNK), lambda g, n: (g, 0, n, 0, 0)),
            pl.BlockSpec((1, HEAD_DIM), lambda g, n: (0, 0)),
        ],
        out_specs=pl.BlockSpec((tb, vh * HEAD_DIM), lambda g, n: (n, g)),
        out_shape=jax.ShapeDtypeStruct((t, 2 * nkh * HEAD_DIM), BF16),
        scratch_shapes=[pltpu.VMEM((vh, HEAD_DIM, HEAD_DIM), F32)],
        compiler_params=_params(("arbitrary", "arbitrary"), 48),
        name="gdn_chunk",
    )(proj, proj, proj, proj, col, row, norm_w.reshape(1, HEAD_DIM))


def _top2_of_4(a, b, c, d):
    hi1, lo1 = jnp.maximum(a, b), jnp.minimum(a, b)
    hi2, lo2 = jnp.maximum(c, d), jnp.minimum(c, d)
    return jnp.maximum(hi1, hi2), jnp.maximum(jnp.minimum(hi1, hi2), jnp.maximum(lo1, lo2))


def _first_argmax(vals):
    best, arg = vals[0], jnp.zeros(vals[0].shape, jnp.int32)
    for n in range(1, len(vals)):
        upd = vals[n] > best
        arg = jnp.where(upd, n, arg)
        best = jnp.where(upd, vals[n], best)
    return best, arg


def _router_kernel(x_ref, w_ref, b_ref, idx_ref, gate_ref, rank_ref, cnt_ref, run, *, tm):
    @pl.when(pl.program_id(0) == 0)
    def _():
        run[...] = jnp.zeros(run.shape, F32)

    xh, xl = _split_bf16(x_ref[...])
    wh, wl = _split_bf16(w_ref[...])
    logits = _dot_nt(wh, xh) + _dot_nt(wh, xl) + _dot_nt(wl, xh) + b_ref[...]
    e = jnp.exp(logits - jnp.max(logits, axis=0, keepdims=True))
    p = e / jnp.sum(e, axis=0, keepdims=True)
    rows = [p[r:r + 1, :] for r in range(N_EXPERTS)]
    scores = []
    for g in range(N_GROUPS):
        t1, t2 = _top2_of_4(*rows[g * EXPERTS_PER_GROUP:(g + 1) * EXPERTS_PER_GROUP])
        scores.append(t1 + t2)
    _, grp = _first_argmax(scores)
    vals = []
    for j in range(EXPERTS_PER_GROUP):
        v = rows[(N_GROUPS - 1) * EXPERTS_PER_GROUP + j]
        for g in range(N_GROUPS - 2, -1, -1):
            v = jnp.where(grp == g, rows[g * EXPERTS_PER_GROUP + j], v)
        vals.append(v)
    v1, i1 = _first_argmax(vals)
    v2, i2 = _first_argmax([jnp.where(i1 == j, -1.0, vals[j]) for j in range(EXPERTS_PER_GROUP)])
    e1 = grp * EXPERTS_PER_GROUP + i1
    e2 = grp * EXPERTS_PER_GROUP + i2
    den = v1 + v2
    idx_ref[0:1, :] = e1
    idx_ref[1:2, :] = e2
    gate_ref[0:1, :] = v1 / den
    gate_ref[1:2, :] = v2 / den

    eio = lax.broadcasted_iota(jnp.int32, (N_EXPERTS, tm), 0)
    hit1 = eio == e1
    hit2 = eio == e2
    onehot = jnp.where(hit1 | hit2, 1.0, 0.0)
    r = lax.broadcasted_iota(jnp.int32, (tm, tm), 0)
    c = lax.broadcasted_iota(jnp.int32, (tm, tm), 1)
    before = jnp.where(r < c, 1.0, 0.0).astype(BF16)
    prefix = _dot(onehot.astype(BF16), before) + run[:, 0:1]
    rank_ref[0:1, :] = jnp.sum(jnp.where(hit1, prefix, 0.0), axis=0, keepdims=True).astype(jnp.int32)
    rank_ref[1:2, :] = jnp.sum(jnp.where(hit2, prefix, 0.0), axis=0, keepdims=True).astype(jnp.int32)
    run[...] = run[...] + jnp.sum(onehot, axis=1, keepdims=True)
    cnt_ref[...] = run[...].astype(jnp.int32)


def moe_router(x, router_w, router_b, *, tm=512):
    t, d = x.shape
    tm = min(tm, t)
    out2 = lambda dt: jax.ShapeDtypeStruct((TOP_K, t), dt)
    blk2 = pl.BlockSpec((TOP_K, tm), lambda i: (0, i))
    return pl.pallas_call(
        functools.partial(_router_kernel, tm=tm),
        grid=(t // tm,),
        in_specs=[
            pl.BlockSpec((tm, d), lambda i: (i, 0)),
            pl.BlockSpec((N_EXPERTS, d), lambda i: (0, 0)),
            pl.BlockSpec((N_EXPERTS, 1), lambda i: (0, 0)),
        ],
        out_specs=[blk2, blk2, blk2, pl.BlockSpec((N_EXPERTS, 128), lambda i: (0, 0))],
        out_shape=[out2(jnp.int32), out2(F32), out2(jnp.int32),
                   jax.ShapeDtypeStruct((N_EXPERTS, 128), jnp.int32)],
        scratch_shapes=[pltpu.VMEM((N_EXPERTS, 128), F32)],
        compiler_params=_params(("arbitrary",), 32),
        name="moe_router",
    )(x, router_w.T, router_b.reshape(N_EXPERTS, 1))


def _dispatch_kernel(dest_ref, pend_ref, x_ref, xs_hbm, zbuf, sem, zsem, *, td, t, tm):
    base = pl.program_id(0) * td

    def zero_tile_copy(start):
        return pltpu.make_async_copy(zbuf, xs_hbm.at[pl.ds(pl.multiple_of(start, tm), tm)], zsem)

    @pl.when(pl.program_id(0) == 0)
    def _():
        zbuf[...] = jnp.zeros(zbuf.shape, zbuf.dtype)
        n_rows = xs_hbm.shape[0]
        starts, valid = [], []
        for e in range(N_EXPERTS):
            starts.append(pend_ref[e] - tm)
            valid.append(pend_ref[e] > (pend_ref[e - 1] if e else 0))
            starts.append(pend_ref[N_EXPERTS - 1] + e * tm)
            valid.append(starts[-1] < n_rows)
        for start, ok in zip(starts, valid):
            @pl.when(ok)
            def _():
                zero_tile_copy(start).start()
        for start, ok in zip(starts, valid):
            @pl.when(ok)
            def _():
                zero_tile_copy(start).wait()

    def row_copy(src_row, dst_row):
        return pltpu.make_async_copy(x_ref.at[pl.ds(src_row, 1)], xs_hbm.at[pl.ds(dst_row, 1)], sem)

    def issue(n, carry):
        for k in range(TOP_K):
            row_copy(n, dest_ref[k * t + base + n]).start()
        return carry

    def drain(n, carry):
        for k in range(TOP_K):
            row_copy(0, 0).wait()
        return carry

    lax.fori_loop(0, td, issue, 0)
    lax.fori_loop(0, td, drain, 0)


def moe_dispatch(x, dest_flat, pend, n_rows, *, tm, td=512):
    t, d = x.shape
    td = min(td, t)
    return pl.pallas_call(
        functools.partial(_dispatch_kernel, td=td, t=t, tm=tm),
        grid_spec=pltpu.PrefetchScalarGridSpec(
            num_scalar_prefetch=2,
            grid=(t // td,),
            in_specs=[pl.BlockSpec((td, d), lambda i, dest, pend: (i, 0))],
            out_specs=pl.BlockSpec(memory_space=pl.ANY),
            scratch_shapes=[pltpu.VMEM((tm, d), x.dtype), pltpu.SemaphoreType.DMA(()),
                            pltpu.SemaphoreType.DMA(())],
        ),
        out_shape=jax.ShapeDtypeStruct((n_rows, d), x.dtype),
        compiler_params=_params(("arbitrary",), 32),
        name="moe_dispatch",
    )(dest_flat, pend, x)


def _moe_ffn_kernel(te_ref, nu_ref, x_ref, wg_ref, wu_ref, wd_ref, y_ref, *, tf):
    del te_ref

    @pl.when(pl.program_id(0) >= nu_ref[0])
    def _():
        y_ref[...] = jnp.zeros(y_ref.shape, F32)

    @pl.when(pl.program_id(0) < nu_ref[0])
    def _():
        x = x_ref[...].astype(BF16)
        f_dim = wg_ref.shape[1]
        acc = None
        for f in range(f_dim // tf):
            cols = slice(f * tf, (f + 1) * tf)
            h = _silu(_dot(x, wg_ref[:, cols])) * _dot(x, wu_ref[:, cols])
            part = _dot(h.astype(BF16), wd_ref[cols, :])
            acc = part if acc is None else acc + part
        y_ref[...] = acc


def moe_ffn(xs, tile_expert, n_used, wg, wu, wd, layer, *, tm, tf=512):
    n_rows, d = xs.shape
    f_dim = wg.shape[3]
    tf = min(tf, f_dim)
    row_map = lambda i, te, nu: (jnp.minimum(i, nu[0] - 1), 0)
    w_map = lambda i, te, nu: (layer, te[i], 0, 0)
    return pl.pallas_call(
        functools.partial(_moe_ffn_kernel, tf=tf),
        grid_spec=pltpu.PrefetchScalarGridSpec(
            num_scalar_prefetch=2,
            grid=(n_rows // tm,),
            in_specs=[
                pl.BlockSpec((tm, d), row_map),
                pl.BlockSpec((None, None, d, f_dim), w_map),
                pl.BlockSpec((None, None, d, f_dim), w_map),
                pl.BlockSpec((None, None, f_dim, d), w_map),
            ],
            out_specs=pl.BlockSpec((tm, d), lambda i, te, nu: (i, 0)),
        ),
        out_shape=jax.ShapeDtypeStruct((n_rows, d), F32),
        compiler_params=_params(("arbitrary",), 58),
        name="moe_ffn",
    )(tile_expert, n_used, xs, wg, wu, wd)


def _combine_kernel(dest_ref, y_hbm, x_ref, gt_ref, g_ref, b_ref, o_ref, ob_ref, buf, sem,
                    *, tc, n_tiles, t):
    i = pl.program_id(0)

    def row_copy(src_row, slot, k, n):
        return pltpu.make_async_copy(y_hbm.at[pl.ds(src_row, 1)], buf.at[slot, k, pl.ds(n, 1)],
                                     sem.at[slot])

    def issue(tile, slot):
        def body(n, carry):
            for k in range(TOP_K):
                row_copy(dest_ref[k * t + tile * tc + n], slot, k, n).start()
            return carry
        lax.fori_loop(0, tc, body, 0)

    @pl.when(i == 0)
    def _():
        issue(0, 0)

    @pl.when(i + 1 < n_tiles)
    def _():
        issue(i + 1, (i + 1) % 2)

    slot = i % 2

    def drain(n, carry):
        for k in range(TOP_K):
            row_copy(0, slot, k, 0).wait()
        return carry

    lax.fori_loop(0, tc, drain, 0)
    gt = gt_ref[...]
    mix = gt[:, 0:1] * buf[slot, 0] + gt[:, 1:2] * buf[slot, 1]
    out = _layer_norm(DEEPNORM_ALPHA * x_ref[...] + mix, g_ref[...], b_ref[...])
    o_ref[...] = out
    ob_ref[...] = out.astype(BF16)


def moe_combine(y, dest_flat, x, gates_t, ln_g, ln_b, *, tc=256):
    t, d = x.shape
    tc = min(tc, t)
    n_tiles = t // tc
    row = lambda v: v.reshape(1, d)
    blk = lambda i, dest: (i, 0)
    fixed = lambda i, dest: (0, 0)
    return pl.pallas_call(
        functools.partial(_combine_kernel, tc=tc, n_tiles=n_tiles, t=t),
        grid_spec=pltpu.PrefetchScalarGridSpec(
            num_scalar_prefetch=1,
            grid=(n_tiles,),
            in_specs=[
                pl.BlockSpec(memory_space=pl.ANY),
                pl.BlockSpec((tc, d), blk),
                pl.BlockSpec((tc, TOP_K), blk),
                pl.BlockSpec((1, d), fixed),
                pl.BlockSpec((1, d), fixed),
            ],
            out_specs=[pl.BlockSpec((tc, d), blk), pl.BlockSpec((tc, d), blk)],
            scratch_shapes=[pltpu.VMEM((2, TOP_K, tc, d), F32), pltpu.SemaphoreType.DMA((2,))],
        ),
        out_shape=[jax.ShapeDtypeStruct((t, d), F32), jax.ShapeDtypeStruct((t, d), BF16)],
        compiler_params=_params(("arbitrary",), 40),
        name="moe_combine",
    )(dest_flat, y, x, gates_t, row(ln_g), row(ln_b))


def moe_layer(x, router_w, router_b, wg, wu, wd, layer, ln_g, ln_b, *, tm=256):
    t, d = x.shape
    idx, gates, rank, cnt = moe_router(x, router_w, router_b)
    counts = cnt[:, 0]
    padded = (counts + tm - 1) // tm * tm
    pend = jnp.cumsum(padded).astype(jnp.int32)
    pstart = pend - padded
    experts = jnp.arange(N_EXPERTS, dtype=jnp.int32)
    seg_start = jnp.sum(jnp.where(idx[:, :, None] == experts, pstart, 0), axis=-1)
    dest = (seg_start + rank).reshape(TOP_K * t).astype(jnp.int32)
    n_tiles = (t * TOP_K) // tm + N_EXPERTS
    n_used = pend[-1] // tm
    tile_start = jnp.arange(n_tiles, dtype=jnp.int32) * tm
    te = jnp.sum((pend[None, :] <= tile_start[:, None]).astype(jnp.int32), axis=1)
    last_used = jnp.sum((pend <= (n_used - 1) * tm).astype(jnp.int32))
    te = jnp.where(tile_start < pend[-1], te, last_used).astype(jnp.int32)

    xs = moe_dispatch(x, dest, pend, n_tiles * tm, tm=tm)
    y = moe_ffn(xs, te, n_used.reshape(1), wg, wu, wd, layer, tm=tm)
    return moe_combine(y, dest, x, gates.T, ln_g, ln_b)


def kernel(x, conv_w_pw1, conv_b_pw1, conv_w_dw, conv_b_dw, conv_ln_g, conv_ln_b, conv_w_pw2,
           conv_b_pw2, gdn_w_in, gdn_w_conv, gdn_a_log, gdn_dt_bias, gdn_norm_w, gdn_w_out,
           router_w, router_b, moe_w_gate, moe_w_up, moe_w_down, ln_mix_g, ln_mix_b, ln_ffn_g,
           ln_ffn_b):
    b_, s_, d = x.shape
    t = b_ * s_
    xf = x.reshape(t, d)
    xb = xf.astype(BF16)
    wg_b = moe_w_gate.astype(BF16)
    wu_b = moe_w_up.astype(BF16)
    wd_b = moe_w_down.astype(BF16)
    n_main = 6 * d
    n_vh = gdn_a_log.shape[1]
    kg = min(GDN_KEY_HEADS_PER_STEP, n_vh // 2)
    for i in range(DEPTH):
        j = i // 2
        if i % 2 == 0:
            h = pw1_glu(xb, conv_w_pw1, conv_b_pw1[j], j)
            hb = dwconv_ln_silu(h, conv_w_dw[j], conv_b_dw[j], conv_ln_g[j], conv_ln_b[j])
            xf, xb = mm_res_ln(hb, conv_w_pw2, j, conv_b_pw2[j], xf, ln_mix_g[i], ln_mix_b[i])
        else:
            proj = gdn_in_proj(xb, gdn_w_in, j, gdn_w_conv[j])
            col, gct = gdn_gates(xf, gdn_w_in[j, :, n_main:n_main + n_vh],
                                 gdn_w_in[j, :, n_main + n_vh:], gdn_a_log[j], gdn_dt_bias[j],
                                 hg=2 * kg)
            ob = gdn_chunk(proj, col, gct, gdn_norm_w[j], kg=kg)
            xf, xb = mm_res_ln(ob, gdn_w_out, j, jnp.zeros((d,), F32), xf, ln_mix_g[i], ln_mix_b[i])
        xf, xb = moe_layer(xf, router_w, router_b, wg_b, wu_b, wd_b, i, ln_ffn_g[i], ln_ffn_b[i])
    return xf.reshape(b_, s_, d)
```

```python
import functools

import jax
import jax.numpy as jnp
from jax import lax
from jax.experimental import pallas as pl
from jax.experimental.pallas import tpu as pltpu

F32 = jnp.float32
BF16 = jnp.bfloat16

DEPTH = 4
DEEPNORM_ALPHA = (2 * DEPTH) ** 0.25
LN_EPS = 1e-5
NORM_EPS = 1e-6
HEAD_DIM = 128
GDN_CHUNK = 64
N_EXPERTS = 16
N_GROUPS = 4
EXPERTS_PER_GROUP = N_EXPERTS // N_GROUPS
TOP_K = 2
SUBLANES = 8
HALO_ROWS = 32
CARRY_ROWS = 8
DMA_ISSUE_UNROLL = 8
GDN_KEY_HEADS_PER_STEP = 4
MIB = 1024 * 1024


def _params(semantics, vmem_mib):
    return pltpu.CompilerParams(dimension_semantics=semantics, vmem_limit_bytes=vmem_mib * MIB)


def _sigmoid(x):
    return 1.0 / (1.0 + jnp.exp(-x))


def _silu(x):
    return x * _sigmoid(x)


def _layer_norm(y, g, b):
    mu = jnp.mean(y, axis=-1, keepdims=True)
    d = y - mu
    var = jnp.mean(d * d, axis=-1, keepdims=True)
    return d * lax.rsqrt(var + LN_EPS) * g + b


def _split_bf16(x):
    hi = x.astype(BF16)
    lo = (x - hi.astype(F32)).astype(BF16)
    return hi, lo


def _dot(a, b):
    return jnp.dot(a, b, preferred_element_type=F32)


def _dot_nt(a, b):
    return lax.dot_general(a, b, (((1,), (1,)), ((), ())), preferred_element_type=F32)


def _dot_tn(a, b):
    return lax.dot_general(a, b, (((0,), (0,)), ((), ())), preferred_element_type=F32)


def _dot3(x, w):
    xh, xl = _split_bf16(x)
    wh, wl = _split_bf16(w)
    return _dot(xh, wh) + _dot(xl, wh) + _dot(xh, wl)


def _pw1_glu_kernel(x_ref, wv_ref, wg_ref, bv_ref, bg_ref, o_ref, wv_s, wg_s):
    @pl.when(pl.program_id(1) == 0)
    def _():
        wv_s[...] = wv_ref[...].astype(BF16)
        wg_s[...] = wg_ref[...].astype(BF16)

    x = x_ref[...]
    val = _dot(x, wv_s[...]) + bv_ref[...]
    gate = _dot(x, wg_s[...]) + bg_ref[...]
    o_ref[...] = val * _sigmoid(gate)


def pw1_glu(xb, w_all, b, layer, *, tm=1024, tn=512):
    m, k = xb.shape
    n = w_all.shape[2] // 2
    tm, tn = min(tm, m), min(tn, n)
    nj = n // tn
    b2 = b.reshape(1, 2 * n)
    w = w_all
    return pl.pallas_call(
        _pw1_glu_kernel,
        grid=(nj, m // tm),
        in_specs=[
            pl.BlockSpec((tm, k), lambda j, i: (i, 0)),
            pl.BlockSpec((None, k, tn), lambda j, i: (layer, 0, j)),
            pl.BlockSpec((None, k, tn), lambda j, i: (layer, 0, j + nj)),
            pl.BlockSpec((1, tn), lambda j, i: (0, j)),
            pl.BlockSpec((1, tn), lambda j, i: (0, j + nj)),
        ],
        out_specs=pl.BlockSpec((tm, tn), lambda j, i: (i, j)),
        out_shape=jax.ShapeDtypeStruct((m, n), F32),
        scratch_shapes=[pltpu.VMEM((k, tn), BF16), pltpu.VMEM((k, tn), BF16)],
        compiler_params=_params(("arbitrary", "arbitrary"), 48),
        name="pw1_glu",
    )(xb, w, w, b2, b2)


def _dwconv_ln_silu_kernel(h_ref, halo_ref, w_ref, b_ref, g_ref, beta_ref, o_ref, buf, cbuf,
                           *, ts, width, cw, rb):
    i = pl.program_id(0)
    d = h_ref.shape[1]
    buf[0:HALO_ROWS, :] = jnp.where(i > 0, halo_ref[...], 0.0)
    buf[HALO_ROWS:HALO_ROWS + ts, :] = h_ref[...]

    def col_body(c, carry):
        cols = pl.ds(pl.multiple_of(c * cw, cw), cw)
        for r in range(ts // rb):
            acc = jnp.broadcast_to(b_ref[:, cols], (rb, cw))
            for b in range(SUBLANES):
                yb = None
                for a in range((width - 1 - b) // SUBLANES + 1):
                    j = width - 1 - (SUBLANES * a + b)
                    start = r * rb + HALO_ROWS - SUBLANES * (a + 1)
                    term = w_ref[j:j + 1, cols] * buf[pl.ds(start, rb + SUBLANES), cols]
                    yb = term if yb is None else yb + term
                acc = acc + yb[SUBLANES - b:SUBLANES - b + rb]
            cbuf[pl.ds(r * rb, rb), cols] = acc
        return carry

    lax.fori_loop(0, d // cw, col_body, 0)
    y = _layer_norm(cbuf[...], g_ref[...], beta_ref[...])
    o_ref[...] = _silu(y).astype(o_ref.dtype)


def dwconv_ln_silu(h, w_dw, b_dw, ln_g, ln_b, *, ts=256, cw=256, rb=64):
    t, d = h.shape
    width = w_dw.shape[0]
    assert SUBLANES * ((width - 1) // SUBLANES + 1) <= HALO_ROWS
    ts, cw = min(ts, t), min(cw, d)
    halo_per_tile = ts // HALO_ROWS
    kern = functools.partial(_dwconv_ln_silu_kernel, ts=ts, width=width, cw=cw, rb=rb)
    row = lambda a: a.reshape(1, d)
    return pl.pallas_call(
        kern,
        grid=(t // ts,),
        in_specs=[
            pl.BlockSpec((ts, d), lambda i: (i, 0)),
            pl.BlockSpec((HALO_ROWS, d), lambda i: (jnp.maximum(i * halo_per_tile - 1, 0), 0)),
            pl.BlockSpec((width, d), lambda i: (0, 0)),
            pl.BlockSpec((1, d), lambda i: (0, 0)),
            pl.BlockSpec((1, d), lambda i: (0, 0)),
            pl.BlockSpec((1, d), lambda i: (0, 0)),
        ],
        out_specs=pl.BlockSpec((ts, d), lambda i: (i, 0)),
        out_shape=jax.ShapeDtypeStruct((t, d), BF16),
        scratch_shapes=[pltpu.VMEM((ts + HALO_ROWS, d), F32), pltpu.VMEM((ts, d), F32)],
        compiler_params=_params(("arbitrary",), 40),
        name="dwconv_ln_silu",
    )(h, h, w_dw, row(b_dw), row(ln_g), row(ln_b))


def _mm_res_ln_kernel(a_ref, w_hbm, bias_ref, res_ref, g_ref, b_ref, o_ref, ob_ref, wres, stage,
                      sem, *, layer, tk, sub):
    nk = wres.shape[0] // tk

    @pl.when(pl.program_id(0) == 0)
    def _():
        def chunk_copy(c):
            return pltpu.make_async_copy(w_hbm.at[layer, pl.ds(c * tk, tk)], stage.at[c % 2],
                                         sem.at[c % 2])
        chunk_copy(0).start()
        for c in range(nk):
            if c + 1 < nk:
                chunk_copy(c + 1).start()
            chunk_copy(c).wait()
            wres[c * tk:(c + 1) * tk, :] = stage[c % 2].astype(BF16)

    for sb in range(a_ref.shape[0] // sub):
        rows = slice(sb * sub, (sb + 1) * sub)
        y = DEEPNORM_ALPHA * res_ref[rows, :] + (_dot(a_ref[rows, :], wres[...]) + bias_ref[...])
        out = _layer_norm(y, g_ref[...], b_ref[...])
        o_ref[rows, :] = out
        ob_ref[rows, :] = out.astype(BF16)


def mm_res_ln(a, w_all, layer, bias, res, ln_g, ln_b, *, tm=256, tk=512, sub=128):
    m, k = a.shape
    n = w_all.shape[2]
    tm, tk = min(tm, m), min(tk, k)
    row = lambda v: v.reshape(1, n)
    return pl.pallas_call(
        functools.partial(_mm_res_ln_kernel, layer=layer, tk=tk, sub=min(sub, tm)),
        grid=(m // tm,),
        in_specs=[
            pl.BlockSpec((tm, k), lambda i: (i, 0)),
            pl.BlockSpec(memory_space=pl.ANY),
            pl.BlockSpec((1, n), lambda i: (0, 0)),
            pl.BlockSpec((tm, n), lambda i: (i, 0)),
            pl.BlockSpec((1, n), lambda i: (0, 0)),
            pl.BlockSpec((1, n), lambda i: (0, 0)),
        ],
        out_specs=[pl.BlockSpec((tm, n), lambda i: (i, 0)),
                   pl.BlockSpec((tm, n), lambda i: (i, 0))],
        out_shape=[jax.ShapeDtypeStruct((m, n), F32), jax.ShapeDtypeStruct((m, n), BF16)],
        scratch_shapes=[pltpu.VMEM((k, n), BF16), pltpu.VMEM((2, tk, n), F32),
                        pltpu.SemaphoreType.DMA((2,))],
        compiler_params=_params(("arbitrary",), 48),
        name="mm_res_ln",
    )(a, w_all, row(bias), res, row(ln_g), row(ln_b))


def _gdn_in_kernel(x_ref, w_ref, wc_ref, o_ref, wbf, pbuf, *, tm, tn, sub, nq, nqk, nconv, cwidth):
    j = pl.program_id(0)
    i = pl.program_id(1)
    heads = [slice(hh * HEAD_DIM, (hh + 1) * HEAD_DIM) for hh in range(tn // HEAD_DIM)]

    @pl.when(i == 0)
    def _():
        wbf[...] = w_ref[...].astype(BF16)
        pbuf[0:CARRY_ROWS, :] = jnp.zeros((CARRY_ROWS, tn), F32)

    def for_sub_blocks(epilogue):
        for sb in range(tm // sub):
            rows = slice(sb * sub, (sb + 1) * sub)
            epilogue(sb, rows, _dot(x_ref[rows, :], wbf[...]))

    def conv_silu(sb, p):
        base = CARRY_ROWS + sb * sub
        pbuf[base:base + sub, :] = p
        c = None
        for jj in range(cwidth):
            term = wc_ref[jj:jj + 1, :] * pbuf[pl.ds(base - (cwidth - 1) + jj, sub), :]
            c = term if c is None else c + term
        return _silu(c)

    def store_heads(rows, s):
        for hh, cols in enumerate(heads):
            o_ref[hh, rows, :] = s[:, cols]

    def gate_epilogue(sb, rows, p):
        store_heads(rows, _silu(p))

    def value_epilogue(sb, rows, p):
        store_heads(rows, conv_silu(sb, p))

    def qk_epilogue(sb, rows, p):
        s = conv_silu(sb, p)
        scale = jnp.where(j < nq, HEAD_DIM ** -0.5, 1.0).astype(F32)
        for hh, cols in enumerate(heads):
            blk = s[:, cols]
            ss = jnp.sum(blk * blk, axis=-1, keepdims=True)
            o_ref[hh, rows, :] = blk * lax.rsqrt(ss + NORM_EPS) * scale

    @pl.when(j >= nconv)
    def _():
        for_sub_blocks(gate_epilogue)

    @pl.when((j >= nqk) & (j < nconv))
    def _():
        for_sub_blocks(value_epilogue)
        pbuf[0:CARRY_ROWS, :] = pbuf[tm:tm + CARRY_ROWS, :]

    @pl.when(j < nqk)
    def _():
        for_sub_blocks(qk_epilogue)
        pbuf[0:CARRY_ROWS, :] = pbuf[tm:tm + CARRY_ROWS, :]


def gdn_in_proj(xb, w_all, layer, w_conv, *, tm=1024, tn=512, sub=256):
    t, d = xb.shape
    qk_dim = d
    v_dim = 2 * d
    n_main = 2 * qk_dim + 2 * v_dim
    tm, tn = min(tm, t), min(tn, qk_dim)
    cwidth = w_conv.shape[0]
    nq = qk_dim // tn
    nconv = (2 * qk_dim + v_dim) // tn
    hpt = tn // HEAD_DIM
    kern = functools.partial(_gdn_in_kernel, tm=tm, tn=tn, sub=min(sub, tm), nq=nq, nqk=2 * nq,
                             nconv=nconv, cwidth=cwidth)
    return pl.pallas_call(
        kern,
        grid=(n_main // tn, t // tm),
        in_specs=[
            pl.BlockSpec((tm, d), lambda j, i: (i, 0)),
            pl.BlockSpec((None, d, tn), lambda j, i: (layer, 0, j)),
            pl.BlockSpec((cwidth, tn), lambda j, i: (0, jnp.minimum(j, nconv - 1))),
        ],
        out_specs=pl.BlockSpec((hpt, tm, HEAD_DIM), lambda j, i: (j, i, 0)),
        out_shape=jax.ShapeDtypeStruct((n_main // HEAD_DIM, t, HEAD_DIM), F32),
        scratch_shapes=[pltpu.VMEM((d, tn), BF16), pltpu.VMEM((tm + CARRY_ROWS, tn), F32)],
        compiler_params=_params(("arbitrary", "arbitrary"), 48),
        name="gdn_in_proj",
    )(xb, w_all, w_conv)


def _gdn_gate_kernel(x_ref, wb_ref, wa_ref, alog_ref, dtb_ref, col_ref, gct_ref, *, tm, hg):
    x = x_ref[...]
    beta = _sigmoid(_dot3(x, wb_ref[...]))
    a = _dot3(x, wa_ref[...]) + dtb_ref[...]
    softplus = jnp.maximum(a, 0.0) + jnp.log(1.0 + jnp.exp(-jnp.abs(a)))
    g = -jnp.exp(alog_ref[...]) * softplus
    r = lax.broadcasted_iota(jnp.int32, (tm, tm), 0)
    c = lax.broadcasted_iota(jnp.int32, (tm, tm), 1)
    same = r // GDN_CHUNK == c // GDN_CHUNK
    lower = jnp.where(same & (c <= r), 1.0, 0.0).astype(BF16)
    upper = jnp.where(same & (r <= c), 1.0, 0.0).astype(BF16)
    g1 = g.astype(BF16)
    r1 = g - g1.astype(F32)
    g2 = r1.astype(BF16)
    g3 = (r1 - g2.astype(F32)).astype(BF16)
    gc = _dot(lower, g1) + _dot(lower, g2) + _dot(lower, g3)
    gct_ref[...] = _dot_tn(g1, upper) + _dot_tn(g2, upper) + _dot_tn(g3, upper)
    for grp in range(col_ref.shape[0]):
        cols = slice(grp * hg, (grp + 1) * hg)
        col_ref[grp] = jnp.concatenate([beta[:, cols], gc[:, cols]], axis=1)


def gdn_gates(x, w_beta, w_a, a_log, dt_bias, *, hg, tm=512):
    t, d = x.shape
    h = w_beta.shape[1]
    tm = min(tm, t)
    row = lambda v: v.reshape(1, h)
    return pl.pallas_call(
        functools.partial(_gdn_gate_kernel, tm=tm, hg=hg),
        grid=(t // tm,),
        in_specs=[
            pl.BlockSpec((tm, d), lambda i: (i, 0)),
            pl.BlockSpec((d, h), lambda i: (0, 0)),
            pl.BlockSpec((d, h), lambda i: (0, 0)),
            pl.BlockSpec((1, h), lambda i: (0, 0)),
            pl.BlockSpec((1, h), lambda i: (0, 0)),
        ],
        out_specs=[pl.BlockSpec((h // hg, tm, 2 * hg), lambda i: (0, i, 0)),
                   pl.BlockSpec((h, tm), lambda i: (0, i))],
        out_shape=[jax.ShapeDtypeStruct((h // hg, t, 2 * hg), F32),
                   jax.ShapeDtypeStruct((h, t), F32)],
        compiler_params=_params(("arbitrary",), 32),
        name="gdn_gates",
    )(x, w_beta, w_a, row(a_log), row(dt_bias))


def _gdn_chunk_kernel(q_ref, k_ref, v_ref, z_ref, col_ref, row_ref, nw_ref, o_ref, state,
                      *, nc, kg):
    c_ = GDN_CHUNK
    hd = HEAD_DIM
    vh = 2 * kg

    @pl.when(pl.program_id(1) == 0)
    def _():
        state[...] = jnp.zeros(state.shape, F32)

    ii = lax.broadcasted_iota(jnp.int32, (c_, c_), 0)
    jj = lax.broadcasted_iota(jnp.int32, (c_, c_), 1)
    tril = ii >= jj
    strict = ii > jj

    q = q_ref[...].reshape(kg * nc, c_, hd)
    k = k_ref[...].reshape(kg * nc, c_, hd)
    kq = lax.dot_general(jnp.concatenate([k, q], axis=1).astype(BF16), k.astype(BF16),
                         (((2,), (2,)), ((0,), (0,))), preferred_element_type=F32)
    col = col_ref[...]
    neg_a, rhs, qk, k_dec, q_dec, g_last = [], [], [], [], [], []
    for h in range(vh):
        own = slice((h // 2) * nc, (h // 2 + 1) * nc)
        bcol = col[:, h:h + 1].reshape(nc, c_, 1)
        gcol = col[:, vh + h:vh + h + 1].reshape(nc, c_, 1)
        grow = row_ref[h]
        decay = jnp.where(tril, jnp.exp(jnp.minimum(gcol - grow, 0.0)), 0.0)
        neg_a.append(jnp.where(strict, -(bcol * kq[own, :c_] * decay), 0.0))
        qk.append((kq[own, c_:] * decay).astype(BF16))
        eg = jnp.exp(gcol)
        glast = gcol[:, c_ - 1:c_, :]
        rhs.append(jnp.concatenate([v_ref[h].reshape(nc, c_, hd) * bcol, k[own] * (bcol * eg)],
                                   axis=2))
        k_dec.append((k[own] * jnp.exp(glast - gcol)).astype(BF16))
        q_dec.append(q[own] * eg)
        g_last.append(jnp.exp(glast))

    def chunk_major(per_head):
        stacked = jnp.stack(per_head, axis=1)
        return stacked.reshape((nc * vh,) + stacked.shape[2:])

    pw = chunk_major(neg_a)
    rhs_all = chunk_major(rhs)
    half = (nc * vh) // 2
    n_steps = c_.bit_length() - 1
    bmm = lambda a, b: jnp.einsum("bij,bjk->bik", a, b, preferred_element_type=F32)
    row2 = lax.broadcasted_iota(jnp.int32, (c_, 2 * c_), 0)
    lane2 = lax.broadcasted_iota(jnp.int32, (c_, 2 * c_), 1)
    lane_lo = lane2 < c_
    eye2 = jnp.where((row2 == lane2) | (row2 + c_ == lane2), 1.0, 0.0)

    def block_diag(packed):
        return jnp.concatenate([jnp.where(lane_lo, packed, 0.0), jnp.where(lane_lo, 0.0, packed)],
                               axis=1).astype(BF16)

    q_pow = jnp.concatenate([pw[:half], pw[half:]], axis=2)
    t_inv = eye2 + q_pow
    for s in range(1, n_steps):
        q_pow = bmm(q_pow.astype(BF16), block_diag(q_pow))
        t_inv = t_inv + bmm(q_pow.astype(BF16), block_diag(t_inv))
    stacked_rhs = jnp.concatenate([rhs_all[:half], rhs_all[half:]], axis=1)
    corr = bmm(block_diag(t_inv - eye2), stacked_rhs.astype(BF16))
    sol = rhs_all + jnp.concatenate([corr[:, :c_], corr[:, c_:]], axis=0)
    u = sol[:, :, :hd]
    wq = jnp.concatenate([sol[:, :, hd:], chunk_major(q_dec)], axis=1).astype(BF16)
    qk = chunk_major(qk)
    k_dec = chunk_major(k_dec)
    g_last = chunk_major(g_last)

    nw = nw_ref[...]
    s_cur = [state[h] for h in range(vh)]
    for c in range(nc):
        rows = slice(c * c_, (c + 1) * c_)
        ws = [_dot(wq[c * vh + h], s_cur[h].astype(BF16)) for h in range(vh)]
        v_new = [(u[c * vh + h] - ws[h][:c_]).astype(BF16) for h in range(vh)]
        o = [ws[h][c_:] + _dot(qk[c * vh + h], v_new[h]) for h in range(vh)]
        s_cur = [s_cur[h] * g_last[c * vh + h] + _dot_tn(k_dec[c * vh + h], v_new[h])
                 for h in range(vh)]
        for h in range(vh):
            on = o[h] * lax.rsqrt(jnp.mean(o[h] * o[h], axis=-1, keepdims=True) + NORM_EPS) * nw
            o_ref[rows, h * hd:(h + 1) * hd] = (on * z_ref[h, rows, :]).astype(o_ref.dtype)
    for h in range(vh):
        state[h] = s_cur[h]


def gdn_chunk(proj, col, gct, norm_w, *, kg, tb=512):
    t = proj.shape[1]
    nkh = proj.shape[0] // 6
    vh = 2 * kg
    tb = min(tb, t)
    nc = tb // GDN_CHUNK
    n_chunks = t // GDN_CHUNK
    row = gct.reshape(nkh // kg, vh, n_chunks, 1, GDN_CHUNK)
    return pl.pallas_call(
        functools.partial(_gdn_chunk_kernel, nc=nc, kg=kg),
        grid=(nkh // kg, t // tb),
        in_specs=[
            pl.BlockSpec((kg, tb, HEAD_DIM), lambda g, n: (g, n, 0)),
            pl.BlockSpec((kg, tb, HEAD_DIM), lambda g, n: (nkh // kg + g, n, 0)),
            pl.BlockSpec((vh, tb, HEAD_DIM), lambda g, n: (2 * nkh // vh + g, n, 0)),
            pl.BlockSpec((vh, tb, HEAD_DIM), lambda g, n: (4 * nkh // vh + g, n, 0)),
            pl.BlockSpec((None, tb, 2 * vh), lambda g, n: (g, n, 0)),
            pl.BlockSpec((None, vh, nc, 1, GDN_CHUNK), lambda g, n: (g, 0, n, 0, 0)),
            pl.BlockSpec((1, HEAD_DIM), lambda g, n: (0, 0)),
        ],
        out_specs=pl.BlockSpec((tb, vh * HEAD_DIM), lambda g, n: (n, g)),
        out_shape=jax.ShapeDtypeStruct((t, 2 * nkh * HEAD_DIM), BF16),
        scratch_shapes=[pltpu.VMEM((vh, HEAD_DIM, HEAD_DIM), F32)],
        compiler_params=_params(("arbitrary", "arbitrary"), 48),
        name="gdn_chunk",
    )(proj, proj, proj, proj, col, row, norm_w.reshape(1, HEAD_DIM))


def _top2_of_4(a, b, c, d):
    hi1, lo1 = jnp.maximum(a, b), jnp.minimum(a, b)
    hi2, lo2 = jnp.maximum(c, d), jnp.minimum(c, d)
    return jnp.maximum(hi1, hi2), jnp.maximum(jnp.minimum(hi1, hi2), jnp.maximum(lo1, lo2))


def _first_argmax(vals):
    best, arg = vals[0], jnp.zeros(vals[0].shape, jnp.int32)
    for n in range(1, len(vals)):
        upd = vals[n] > best
        arg = jnp.where(upd, n, arg)
        best = jnp.where(upd, vals[n], best)
    return best, arg


def _router_kernel(x_ref, w_ref, b_ref, idx_ref, gate_ref, rank_ref, cnt_ref, run, *, tm):
    @pl.when(pl.program_id(0) == 0)
    def _():
        run[...] = jnp.zeros(run.shape, F32)

    xh, xl = _split_bf16(x_ref[...])
    wh, wl = _split_bf16(w_ref[...])
    logits = _dot_nt(wh, xh) + _dot_nt(wh, xl) + _dot_nt(wl, xh) + b_ref[...]
    e = jnp.exp(logits - jnp.max(logits, axis=0, keepdims=True))
    p = e / jnp.sum(e, axis=0, keepdims=True)
    rows = [p[r:r + 1, :] for r in range(N_EXPERTS)]
    scores = []
    for g in range(N_GROUPS):
        t1, t2 = _top2_of_4(*rows[g * EXPERTS_PER_GROUP:(g + 1) * EXPERTS_PER_GROUP])
        scores.append(t1 + t2)
    _, grp = _first_argmax(scores)
    vals = []
    for j in range(EXPERTS_PER_GROUP):
        v = rows[(N_GROUPS - 1) * EXPERTS_PER_GROUP + j]
        for g in range(N_GROUPS - 2, -1, -1):
            v = jnp.where(grp == g, rows[g * EXPERTS_PER_GROUP + j], v)
        vals.append(v)
    v1, i1 = _first_argmax(vals)
    v2, i2 = _first_argmax([jnp.where(i1 == j, -1.0, vals[j]) for j in range(EXPERTS_PER_GROUP)])
    e1 = grp * EXPERTS_PER_GROUP + i1
    e2 = grp * EXPERTS_PER_GROUP + i2
    den = v1 + v2
    idx_ref[0:1, :] = e1
    idx_ref[1:2, :] = e2
    gate_ref[0:1, :] = v1 / den
    gate_ref[1:2, :] = v2 / den

    eio = lax.broadcasted_iota(jnp.int32, (N_EXPERTS, tm), 0)
    hit1 = eio == e1
    hit2 = eio == e2
    onehot = jnp.where(hit1 | hit2, 1.0, 0.0)
    r = lax.broadcasted_iota(jnp.int32, (tm, tm), 0)
    c = lax.broadcasted_iota(jnp.int32, (tm, tm), 1)
    before = jnp.where(r < c, 1.0, 0.0).astype(BF16)
    prefix = _dot(onehot.astype(BF16), before) + run[:, 0:1]
    rank_ref[0:1, :] = jnp.sum(jnp.where(hit1, prefix, 0.0), axis=0, keepdims=True).astype(jnp.int32)
    rank_ref[1:2, :] = jnp.sum(jnp.where(hit2, prefix, 0.0), axis=0, keepdims=True).astype(jnp.int32)
    run[...] = run[...] + jnp.sum(onehot, axis=1, keepdims=True)
    cnt_ref[...] = run[...].astype(jnp.int32)


def moe_router(x, router_w, router_b, *, tm=512):
    t, d = x.shape
    tm = min(tm, t)
    out2 = lambda dt: jax.ShapeDtypeStruct((TOP_K, t), dt)
    blk2 = pl.BlockSpec((TOP_K, tm), lambda i: (0, i))
    return pl.pallas_call(
        functools.partial(_router_kernel, tm=tm),
        grid=(t // tm,),
        in_specs=[
            pl.BlockSpec((tm, d), lambda i: (i, 0)),
            pl.BlockSpec((N_EXPERTS, d), lambda i: (0, 0)),
            pl.BlockSpec((N_EXPERTS, 1), lambda i: (0, 0)),
        ],
        out_specs=[blk2, blk2, blk2, pl.BlockSpec((N_EXPERTS, 128), lambda i: (0, 0))],
        out_shape=[out2(jnp.int32), out2(F32), out2(jnp.int32),
                   jax.ShapeDtypeStruct((N_EXPERTS, 128), jnp.int32)],
        scratch_shapes=[pltpu.VMEM((N_EXPERTS, 128), F32)],
        compiler_params=_params(("arbitrary",), 32),
        name="moe_router",
    )(x, router_w.T, router_b.reshape(N_EXPERTS, 1))


def _dispatch_kernel(dest_ref, pend_ref, x_ref, xs_hbm, zbuf, sem, zsem, *, td, t, tm):
    base = pl.program_id(0) * td

    def zero_tile_copy(start):
        return pltpu.make_async_copy(zbuf, xs_hbm.at[pl.ds(pl.multiple_of(start, tm), tm)], zsem)

    @pl.when(pl.program_id(0) == 0)
    def _():
        zbuf[...] = jnp.zeros(zbuf.shape, zbuf.dtype)
        n_rows = xs_hbm.shape[0]
        starts, valid = [], []
        for e in range(N_EXPERTS):
            starts.append(pend_ref[e] - tm)
            valid.append(pend_ref[e] > (pend_ref[e - 1] if e else 0))
            starts.append(pend_ref[N_EXPERTS - 1] + e * tm)
            valid.append(starts[-1] < n_rows)
        for start, ok in zip(starts, valid):
            @pl.when(ok)
            def _():
                zero_tile_copy(start).start()
        for start, ok in zip(starts, valid):
            @pl.when(ok)
            def _():
                zero_tile_copy(start).wait()

    def row_copy(src_row, dst_row):
        return pltpu.make_async_copy(x_ref.at[pl.ds(src_row, 1)], xs_hbm.at[pl.ds(dst_row, 1)], sem)

    def issue(n, carry):
        for k in range(TOP_K):
            row_copy(n, dest_ref[k * t + base + n]).start()
        return carry

    lax.fori_loop(0, td, issue, 0, unroll=DMA_ISSUE_UNROLL)
    all_rows = xs_hbm.at[pl.ds(0, TOP_K * td)]
    pltpu.make_async_copy(all_rows, all_rows, sem).wait()


def moe_dispatch(x, dest_flat, pend, n_rows, *, tm, td=512):
    t, d = x.shape
    td = min(td, t)
    return pl.pallas_call(
        functools.partial(_dispatch_kernel, td=td, t=t, tm=tm),
        grid_spec=pltpu.PrefetchScalarGridSpec(
            num_scalar_prefetch=2,
            grid=(t // td,),
            in_specs=[pl.BlockSpec((td, d), lambda i, dest, pend: (i, 0))],
            out_specs=pl.BlockSpec(memory_space=pl.ANY),
            scratch_shapes=[pltpu.VMEM((tm, d), x.dtype), pltpu.SemaphoreType.DMA(()),
                            pltpu.SemaphoreType.DMA(())],
        ),
        out_shape=jax.ShapeDtypeStruct((n_rows, d), x.dtype),
        compiler_params=_params(("arbitrary",), 32),
        name="moe_dispatch",
    )(dest_flat, pend, x)


def _moe_ffn_kernel(te_ref, nu_ref, x_ref, wg_ref, wu_ref, wd_ref, y_ref, *, tf):
    del te_ref

    @pl.when(pl.program_id(0) >= nu_ref[0])
    def _():
        y_ref[...] = jnp.zeros(y_ref.shape, F32)

    @pl.when(pl.program_id(0) < nu_ref[0])
    def _():
        x = x_ref[...].astype(BF16)
        f_dim = wg_ref.shape[1]
        acc = None
        for f in range(f_dim // tf):
            cols = slice(f * tf, (f + 1) * tf)
            h = _silu(_dot(x, wg_ref[:, cols])) * _dot(x, wu_ref[:, cols])
            part = _dot(h.astype(BF16), wd_ref[cols, :])
            acc = part if acc is None else acc + part
        y_ref[...] = acc


def moe_ffn(xs, tile_expert, n_used, wg, wu, wd, layer, *, tm, tf=512):
    n_rows, d = xs.shape
    f_dim = wg.shape[3]
    tf = min(tf, f_dim)
    row_map = lambda i, te, nu: (jnp.minimum(i, nu[0] - 1), 0)
    w_map = lambda i, te, nu: (layer, te[i], 0, 0)
    return pl.pallas_call(
        functools.partial(_moe_ffn_kernel, tf=tf),
        grid_spec=pltpu.PrefetchScalarGridSpec(
            num_scalar_prefetch=2,
            grid=(n_rows // tm,),
            in_specs=[
                pl.BlockSpec((tm, d), row_map),
                pl.BlockSpec((None, None, d, f_dim), w_map),
                pl.BlockSpec((None, None, d, f_dim), w_map),
                pl.BlockSpec((None, None, f_dim, d), w_map),
            ],
            out_specs=pl.BlockSpec((tm, d), lambda i, te, nu: (i, 0)),
        ),
        out_shape=jax.ShapeDtypeStruct((n_rows, d), F32),
        compiler_params=_params(("arbitrary",), 58),
        name="moe_ffn",
    )(tile_expert, n_used, xs, wg, wu, wd)


def _combine_kernel(dest_ref, y_hbm, x_ref, gt_ref, g_ref, b_ref, o_ref, ob_ref, buf, sem,
                    *, tc, n_tiles, t):
    i = pl.program_id(0)

    def row_copy(src_row, slot, k, n):
        return pltpu.make_async_copy(y_hbm.at[pl.ds(src_row, 1)], buf.at[slot, k, pl.ds(n, 1)],
                                     sem.at[slot])

    def issue(tile, slot):
        def body(n, carry):
            for k in range(TOP_K):
                row_copy(dest_ref[k * t + tile * tc + n], slot, k, n).start()
            return carry
        lax.fori_loop(0, tc, body, 0, unroll=DMA_ISSUE_UNROLL)

    @pl.when(i == 0)
    def _():
        issue(0, 0)

    @pl.when(i + 1 < n_tiles)
    def _():
        issue(i + 1, (i + 1) % 2)

    slot = i % 2
    pltpu.make_async_copy(buf.at[slot], buf.at[slot], sem.at[slot]).wait()
    gt = gt_ref[...]
    mix = gt[:, 0:1] * buf[slot, 0] + gt[:, 1:2] * buf[slot, 1]
    out = _layer_norm(DEEPNORM_ALPHA * x_ref[...] + mix, g_ref[...], b_ref[...])
    o_ref[...] = out
    ob_ref[...] = out.astype(BF16)


def moe_combine(y, dest_flat, x, gates_t, ln_g, ln_b, *, tc=256):
    t, d = x.shape
    tc = min(tc, t)
    n_tiles = t // tc
    row = lambda v: v.reshape(1, d)
    blk = lambda i, dest: (i, 0)
    fixed = lambda i, dest: (0, 0)
    return pl.pallas_call(
        functools.partial(_combine_kernel, tc=tc, n_tiles=n_tiles, t=t),
        grid_spec=pltpu.PrefetchScalarGridSpec(
            num_scalar_prefetch=1,
            grid=(n_tiles,),
            in_specs=[
                pl.BlockSpec(memory_space=pl.ANY),
                pl.BlockSpec((tc, d), blk),
                pl.BlockSpec((tc, TOP_K), blk),
                pl.BlockSpec((1, d), fixed),
                pl.BlockSpec((1, d), fixed),
            ],
            out_specs=[pl.BlockSpec((tc, d), blk), pl.BlockSpec((tc, d), blk)],
            scratch_shapes=[pltpu.VMEM((2, TOP_K, tc, d), F32), pltpu.SemaphoreType.DMA((2,))],
        ),
        out_shape=[jax.ShapeDtypeStruct((t, d), F32), jax.ShapeDtypeStruct((t, d), BF16)],
        compiler_params=_params(("arbitrary",), 40),
        name="moe_combine",
    )(dest_flat, y, x, gates_t, row(ln_g), row(ln_b))


def moe_layer(x, router_w, router_b, wg, wu, wd, layer, ln_g, ln_b, *, tm=256):
    t, d = x.shape
    idx, gates, rank, cnt = moe_router(x, router_w, router_b)
    counts = cnt[:, 0]
    padded = (counts + tm - 1) // tm * tm
    pend = jnp.cumsum(padded).astype(jnp.int32)
    pstart = pend - padded
    experts = jnp.arange(N_EXPERTS, dtype=jnp.int32)
    seg_start = jnp.sum(jnp.where(idx[:, :, None] == experts, pstart, 0), axis=-1)
    dest = (seg_start + rank).reshape(TOP_K * t).astype(jnp.int32)
    n_tiles = (t * TOP_K) // tm + N_EXPERTS
    n_used = pend[-1] // tm
    tile_start = jnp.arange(n_tiles, dtype=jnp.int32) * tm
    te = jnp.sum((pend[None, :] <= tile_start[:, None]).astype(jnp.int32), axis=1)
    last_used = jnp.sum((pend <= (n_used - 1) * tm).astype(jnp.int32))
    te = jnp.where(tile_start < pend[-1], te, last_used).astype(jnp.int32)

    xs = moe_dispatch(x, dest, pend, n_tiles * tm, tm=tm)
    y = moe_ffn(xs, te, n_used.reshape(1), wg, wu, wd, layer, tm=tm)
    return moe_combine(y, dest, x, gates.T, ln_g, ln_b)


def kernel(x, conv_w_pw1, conv_b_pw1, conv_w_dw, conv_b_dw, conv_ln_g, conv_ln_b, conv_w_pw2,
           conv_b_pw2, gdn_w_in, gdn_w_conv, gdn_a_log, gdn_dt_bias, gdn_norm_w, gdn_w_out,
           router_w, router_b, moe_w_gate, moe_w_up, moe_w_down, ln_mix_g, ln_mix_b, ln_ffn_g,
           ln_ffn_b):
    b_, s_, d = x.shape
    t = b_ * s_
    xf = x.reshape(t, d)
    xb = xf.astype(BF16)
    wg_b = moe_w_gate.astype(BF16)
    wu_b = moe_w_up.astype(BF16)
    wd_b = moe_w_down.astype(BF16)
    n_main = 6 * d
    n_vh = gdn_a_log.shape[1]
    kg = min(GDN_KEY_HEADS_PER_STEP, n_vh // 2)
    for i in range(DEPTH):
        j = i // 2
        if i % 2 == 0:
            h = pw1_glu(xb, conv_w_pw1, conv_b_pw1[j], j)
            hb = dwconv_ln_silu(h, conv_w_dw[j], conv_b_dw[j], conv_ln_g[j], conv_ln_b[j])
            xf, xb = mm_res_ln(hb, conv_w_pw2, j, conv_b_pw2[j], xf, ln_mix_g[i], ln_mix_b[i])
        else:
            proj = gdn_in_proj(xb, gdn_w_in, j, gdn_w_conv[j])
            col, gct = gdn_gates(xf, gdn_w_in[j, :, n_main:n_main + n_vh],
                                 gdn_w_in[j, :, n_main + n_vh:], gdn_a_log[j], gdn_dt_bias[j],
                                 hg=2 * kg)
            ob = gdn_chunk(proj, col, gct, gdn_norm_w[j], kg=kg)
            xf, xb = mm_res_ln(ob, gdn_w_out, j, jnp.zeros((d,), F32), xf, ln_mix_g[i], ln_mix_b[i])
        xf, xb = moe_layer(xf, router_w, router_b, wg_b, wu_b, wd_b, i, ln_ffn_g[i], ln_ffn_b[i])
    return xf.reshape(b_, s_, d)
```

```python
import functools
import math

import jax
import jax.numpy as jnp
from jax import lax
from jax.experimental import pallas as pl
from jax.experimental.pallas import tpu as pltpu

F32 = jnp.float32
BF16 = jnp.bfloat16

DEPTH = 4
DEEPNORM_ALPHA = (2 * DEPTH) ** 0.25
LN_EPS = 1e-5
NORM_EPS = 1e-6
HEAD_DIM = 128
GDN_CHUNK = 64
N_EXPERTS = 16
N_GROUPS = 4
EXPERTS_PER_GROUP = N_EXPERTS // N_GROUPS
TOP_K = 2
SUBLANES = 8
HALO_ROWS = 32
CARRY_ROWS = 8
DMA_ISSUE_UNROLL = 8
GDN_KEY_HEADS_PER_STEP = 4
MIB = 1024 * 1024


def _params(semantics, vmem_mib):
    return pltpu.CompilerParams(dimension_semantics=semantics, vmem_limit_bytes=vmem_mib * MIB)


def _sigmoid(x):
    return 1.0 / (1.0 + jnp.exp(-x))


def _silu(x):
    return x * _sigmoid(x)


def _layer_norm(y, g, b):
    mu = jnp.mean(y, axis=-1, keepdims=True)
    d = y - mu
    var = jnp.mean(d * d, axis=-1, keepdims=True)
    return d * lax.rsqrt(var + LN_EPS) * g + b


def _split_bf16(x):
    hi = x.astype(BF16)
    lo = (x - hi.astype(F32)).astype(BF16)
    return hi, lo


def _dot(a, b):
    return jnp.dot(a, b, preferred_element_type=F32)


def _dot_nt(a, b):
    return lax.dot_general(a, b, (((1,), (1,)), ((), ())), preferred_element_type=F32)


def _dot_tn(a, b):
    return lax.dot_general(a, b, (((0,), (0,)), ((), ())), preferred_element_type=F32)


def _dot3(x, w):
    xh, xl = _split_bf16(x)
    wh, wl = _split_bf16(w)
    return _dot(xh, wh) + _dot(xl, wh) + _dot(xh, wl)


def _pw1_glu_kernel(x_ref, wv_ref, wg_ref, bv_ref, bg_ref, o_ref, wv_s, wg_s):
    @pl.when(pl.program_id(1) == 0)
    def _():
        wv_s[...] = wv_ref[...].astype(BF16)
        wg_s[...] = wg_ref[...].astype(BF16)

    x = x_ref[...]
    val = _dot(x, wv_s[...]) + bv_ref[...]
    gate = _dot(x, wg_s[...]) + bg_ref[...]
    o_ref[...] = val * _sigmoid(gate)


def pw1_glu(xb, w_all, b, layer, *, tm=1024, tn=512):
    m, k = xb.shape
    n = w_all.shape[2] // 2
    tm, tn = min(tm, m), min(tn, n)
    nj = n // tn
    b2 = b.reshape(1, 2 * n)
    w = w_all
    return pl.pallas_call(
        _pw1_glu_kernel,
        grid=(nj, m // tm),
        in_specs=[
            pl.BlockSpec((tm, k), lambda j, i: (i, 0)),
            pl.BlockSpec((None, k, tn), lambda j, i: (layer, 0, j)),
            pl.BlockSpec((None, k, tn), lambda j, i: (layer, 0, j + nj)),
            pl.BlockSpec((1, tn), lambda j, i: (0, j)),
            pl.BlockSpec((1, tn), lambda j, i: (0, j + nj)),
        ],
        out_specs=pl.BlockSpec((tm, tn), lambda j, i: (i, j)),
        out_shape=jax.ShapeDtypeStruct((m, n), F32),
        scratch_shapes=[pltpu.VMEM((k, tn), BF16), pltpu.VMEM((k, tn), BF16)],
        compiler_params=_params(("arbitrary", "arbitrary"), 48),
        name="pw1_glu",
    )(xb, w, w, b2, b2)


def _dwconv_ln_silu_kernel(h_ref, halo_ref, w_ref, b_ref, g_ref, beta_ref, o_ref, buf, cbuf,
                           *, ts, width, cw, rb):
    i = pl.program_id(0)
    d = h_ref.shape[1]
    buf[0:HALO_ROWS, :] = jnp.where(i > 0, halo_ref[...], 0.0)
    buf[HALO_ROWS:HALO_ROWS + ts, :] = h_ref[...]

    def col_body(c, carry):
        cols = pl.ds(pl.multiple_of(c * cw, cw), cw)
        for r in range(ts // rb):
            acc = jnp.broadcast_to(b_ref[:, cols], (rb, cw))
            for b in range(SUBLANES):
                yb = None
                for a in range((width - 1 - b) // SUBLANES + 1):
                    j = width - 1 - (SUBLANES * a + b)
                    start = r * rb + HALO_ROWS - SUBLANES * (a + 1)
                    term = w_ref[j:j + 1, cols] * buf[pl.ds(start, rb + SUBLANES), cols]
                    yb = term if yb is None else yb + term
                acc = acc + yb[SUBLANES - b:SUBLANES - b + rb]
            cbuf[pl.ds(r * rb, rb), cols] = acc
        return carry

    lax.fori_loop(0, d // cw, col_body, 0)
    y = _layer_norm(cbuf[...], g_ref[...], beta_ref[...])
    o_ref[...] = _silu(y).astype(o_ref.dtype)


def dwconv_ln_silu(h, w_dw, b_dw, ln_g, ln_b, *, ts=256, cw=256, rb=64):
    t, d = h.shape
    width = w_dw.shape[0]
    assert SUBLANES * ((width - 1) // SUBLANES + 1) <= HALO_ROWS
    ts, cw = min(ts, t), min(cw, d)
    halo_per_tile = ts // HALO_ROWS
    kern = functools.partial(_dwconv_ln_silu_kernel, ts=ts, width=width, cw=cw, rb=rb)
    row = lambda a: a.reshape(1, d)
    return pl.pallas_call(
        kern,
        grid=(t // ts,),
        in_specs=[
            pl.BlockSpec((ts, d), lambda i: (i, 0)),
            pl.BlockSpec((HALO_ROWS, d), lambda i: (jnp.maximum(i * halo_per_tile - 1, 0), 0)),
            pl.BlockSpec((width, d), lambda i: (0, 0)),
            pl.BlockSpec((1, d), lambda i: (0, 0)),
            pl.BlockSpec((1, d), lambda i: (0, 0)),
            pl.BlockSpec((1, d), lambda i: (0, 0)),
        ],
        out_specs=pl.BlockSpec((ts, d), lambda i: (i, 0)),
        out_shape=jax.ShapeDtypeStruct((t, d), BF16),
        scratch_shapes=[pltpu.VMEM((ts + HALO_ROWS, d), F32), pltpu.VMEM((ts, d), F32)],
        compiler_params=_params(("arbitrary",), 40),
        name="dwconv_ln_silu",
    )(h, h, w_dw, row(b_dw), row(ln_g), row(ln_b))


def _mm_res_ln_kernel(a_ref, w_hbm, bias_ref, res_ref, g_ref, b_ref, o_ref, ob_ref, wres, stage,
                      sem, *, layer, tk, sub):
    nk = wres.shape[0] // tk

    @pl.when(pl.program_id(0) == 0)
    def _():
        def chunk_copy(c):
            return pltpu.make_async_copy(w_hbm.at[layer, pl.ds(c * tk, tk)], stage.at[c % 2],
                                         sem.at[c % 2])
        chunk_copy(0).start()
        for c in range(nk):
            if c + 1 < nk:
                chunk_copy(c + 1).start()
            chunk_copy(c).wait()
            wres[c * tk:(c + 1) * tk, :] = stage[c % 2].astype(BF16)

    for sb in range(a_ref.shape[0] // sub):
        rows = slice(sb * sub, (sb + 1) * sub)
        y = DEEPNORM_ALPHA * res_ref[rows, :] + (_dot(a_ref[rows, :], wres[...]) + bias_ref[...])
        out = _layer_norm(y, g_ref[...], b_ref[...])
        o_ref[rows, :] = out
        ob_ref[rows, :] = out.astype(BF16)


def mm_res_ln(a, w_all, layer, bias, res, ln_g, ln_b, *, tm=256, tk=512, sub=128):
    m, k = a.shape
    n = w_all.shape[2]
    tm, tk = min(tm, m), min(tk, k)
    row = lambda v: v.reshape(1, n)
    return pl.pallas_call(
        functools.partial(_mm_res_ln_kernel, layer=layer, tk=tk, sub=min(sub, tm)),
        grid=(m // tm,),
        in_specs=[
            pl.BlockSpec((tm, k), lambda i: (i, 0)),
            pl.BlockSpec(memory_space=pl.ANY),
            pl.BlockSpec((1, n), lambda i: (0, 0)),
            pl.BlockSpec((tm, n), lambda i: (i, 0)),
            pl.BlockSpec((1, n), lambda i: (0, 0)),
            pl.BlockSpec((1, n), lambda i: (0, 0)),
        ],
        out_specs=[pl.BlockSpec((tm, n), lambda i: (i, 0)),
                   pl.BlockSpec((tm, n), lambda i: (i, 0))],
        out_shape=[jax.ShapeDtypeStruct((m, n), F32), jax.ShapeDtypeStruct((m, n), BF16)],
        scratch_shapes=[pltpu.VMEM((k, n), BF16), pltpu.VMEM((2, tk, n), F32),
                        pltpu.SemaphoreType.DMA((2,))],
        compiler_params=_params(("arbitrary",), 48),
        name="mm_res_ln",
    )(a, w_all, row(bias), res, row(ln_g), row(ln_b))


def _gdn_in_kernel(x_ref, w_ref, wc_ref, o_ref, wbf, pbuf, *, tm, tn, sub, nq, nqk, nconv, cwidth):
    j = pl.program_id(0)
    i = pl.program_id(1)
    heads = [slice(hh * HEAD_DIM, (hh + 1) * HEAD_DIM) for hh in range(tn // HEAD_DIM)]

    @pl.when(i == 0)
    def _():
        wbf[...] = w_ref[...].astype(BF16)
        pbuf[0:CARRY_ROWS, :] = jnp.zeros((CARRY_ROWS, tn), F32)

    def for_sub_blocks(epilogue):
        for sb in range(tm // sub):
            rows = slice(sb * sub, (sb + 1) * sub)
            epilogue(sb, rows, _dot(x_ref[rows, :], wbf[...]))

    def conv_silu(sb, p):
        base = CARRY_ROWS + sb * sub
        pbuf[base:base + sub, :] = p
        c = None
        for jj in range(cwidth):
            term = wc_ref[jj:jj + 1, :] * pbuf[pl.ds(base - (cwidth - 1) + jj, sub), :]
            c = term if c is None else c + term
        return _silu(c)

    def store_heads(rows, s):
        for hh, cols in enumerate(heads):
            o_ref[hh, rows, :] = s[:, cols]

    def gate_epilogue(sb, rows, p):
        store_heads(rows, _silu(p))

    def value_epilogue(sb, rows, p):
        store_heads(rows, conv_silu(sb, p))

    def qk_epilogue(sb, rows, p):
        s = conv_silu(sb, p)
        scale = jnp.where(j < nq, HEAD_DIM ** -0.5, 1.0).astype(F32)
        for hh, cols in enumerate(heads):
            blk = s[:, cols]
            ss = jnp.sum(blk * blk, axis=-1, keepdims=True)
            o_ref[hh, rows, :] = blk * lax.rsqrt(ss + NORM_EPS) * scale

    @pl.when(j >= nconv)
    def _():
        for_sub_blocks(gate_epilogue)

    @pl.when((j >= nqk) & (j < nconv))
    def _():
        for_sub_blocks(value_epilogue)
        pbuf[0:CARRY_ROWS, :] = pbuf[tm:tm + CARRY_ROWS, :]

    @pl.when(j < nqk)
    def _():
        for_sub_blocks(qk_epilogue)
        pbuf[0:CARRY_ROWS, :] = pbuf[tm:tm + CARRY_ROWS, :]


def gdn_in_proj(xb, w_all, layer, w_conv, *, tm=1024, tn=512, sub=256):
    t, d = xb.shape
    qk_dim = d
    v_dim = 2 * d
    n_main = 2 * qk_dim + 2 * v_dim
    tm, tn = min(tm, t), min(tn, qk_dim)
    cwidth = w_conv.shape[0]
    nq = qk_dim // tn
    nconv = (2 * qk_dim + v_dim) // tn
    hpt = tn // HEAD_DIM
    kern = functools.partial(_gdn_in_kernel, tm=tm, tn=tn, sub=min(sub, tm), nq=nq, nqk=2 * nq,
                             nconv=nconv, cwidth=cwidth)
    return pl.pallas_call(
        kern,
        grid=(n_main // tn, t // tm),
        in_specs=[
            pl.BlockSpec((tm, d), lambda j, i: (i, 0)),
            pl.BlockSpec((None, d, tn), lambda j, i: (layer, 0, j)),
            pl.BlockSpec((cwidth, tn), lambda j, i: (0, jnp.minimum(j, nconv - 1))),
        ],
        out_specs=pl.BlockSpec((hpt, tm, HEAD_DIM), lambda j, i: (j, i, 0)),
        out_shape=jax.ShapeDtypeStruct((n_main // HEAD_DIM, t, HEAD_DIM), F32),
        scratch_shapes=[pltpu.VMEM((d, tn), BF16), pltpu.VMEM((tm + CARRY_ROWS, tn), F32)],
        compiler_params=_params(("arbitrary", "arbitrary"), 48),
        name="gdn_in_proj",
    )(xb, w_all, w_conv)


def _gdn_gate_kernel(x_ref, wb_ref, wa_ref, alog_ref, dtb_ref, col_ref, gct_ref, *, tm, hg):
    x = x_ref[...]
    beta = _sigmoid(_dot3(x, wb_ref[...]))
    a = _dot3(x, wa_ref[...]) + dtb_ref[...]
    softplus = jnp.maximum(a, 0.0) + jnp.log(1.0 + jnp.exp(-jnp.abs(a)))
    g = -jnp.exp(alog_ref[...]) * softplus
    r = lax.broadcasted_iota(jnp.int32, (tm, tm), 0)
    c = lax.broadcasted_iota(jnp.int32, (tm, tm), 1)
    same = r // GDN_CHUNK == c // GDN_CHUNK
    lower = jnp.where(same & (c <= r), 1.0, 0.0).astype(BF16)
    upper = jnp.where(same & (r <= c), 1.0, 0.0).astype(BF16)
    g1 = g.astype(BF16)
    r1 = g - g1.astype(F32)
    g2 = r1.astype(BF16)
    g3 = (r1 - g2.astype(F32)).astype(BF16)
    gc = _dot(lower, g1) + _dot(lower, g2) + _dot(lower, g3)
    gct_ref[...] = _dot_tn(g1, upper) + _dot_tn(g2, upper) + _dot_tn(g3, upper)
    for grp in range(col_ref.shape[0]):
        cols = slice(grp * hg, (grp + 1) * hg)
        col_ref[grp] = jnp.concatenate([beta[:, cols], gc[:, cols]], axis=1)


def gdn_gates(x, w_beta, w_a, a_log, dt_bias, *, hg, tm=512):
    t, d = x.shape
    h = w_beta.shape[1]
    tm = min(tm, t)
    row = lambda v: v.reshape(1, h)
    return pl.pallas_call(
        functools.partial(_gdn_gate_kernel, tm=tm, hg=hg),
        grid=(t // tm,),
        in_specs=[
            pl.BlockSpec((tm, d), lambda i: (i, 0)),
            pl.BlockSpec((d, h), lambda i: (0, 0)),
            pl.BlockSpec((d, h), lambda i: (0, 0)),
            pl.BlockSpec((1, h), lambda i: (0, 0)),
            pl.BlockSpec((1, h), lambda i: (0, 0)),
        ],
        out_specs=[pl.BlockSpec((h // hg, tm, 2 * hg), lambda i: (0, i, 0)),
                   pl.BlockSpec((h, tm), lambda i: (0, i))],
        out_shape=[jax.ShapeDtypeStruct((h // hg, t, 2 * hg), F32),
                   jax.ShapeDtypeStruct((h, t), F32)],
        compiler_params=_params(("arbitrary",), 32),
        name="gdn_gates",
    )(x, w_beta, w_a, row(a_log), row(dt_bias))


def _gdn_chunk_kernel(q_ref, k_ref, v_ref, z_ref, col_ref, row_ref, nw_ref, o_ref, state,
                      *, nc, kg):
    c_ = GDN_CHUNK
    hd = HEAD_DIM
    vh = 2 * kg

    @pl.when(pl.program_id(1) == 0)
    def _():
        state[...] = jnp.zeros(state.shape, F32)

    ii = lax.broadcasted_iota(jnp.int32, (c_, c_), 0)
    jj = lax.broadcasted_iota(jnp.int32, (c_, c_), 1)
    tril = ii >= jj
    strict = ii > jj

    q = q_ref[...].reshape(kg * nc, c_, hd)
    k = k_ref[...].reshape(kg * nc, c_, hd)
    kq = lax.dot_general(jnp.concatenate([k, q], axis=1).astype(BF16), k.astype(BF16),
                         (((2,), (2,)), ((0,), (0,))), preferred_element_type=F32)
    col = col_ref[...]
    neg_a, rhs, qk, k_dec, q_dec, g_last = [], [], [], [], [], []
    for h in range(vh):
        own = slice((h // 2) * nc, (h // 2 + 1) * nc)
        bcol = col[:, h:h + 1].reshape(nc, c_, 1)
        gcol = col[:, vh + h:vh + h + 1].reshape(nc, c_, 1)
        grow = row_ref[h]
        decay = jnp.where(tril, jnp.exp(jnp.minimum(gcol - grow, 0.0)), 0.0)
        neg_a.append(jnp.where(strict, -(bcol * kq[own, :c_] * decay), 0.0))
        qk.append((kq[own, c_:] * decay).astype(BF16))
        eg = jnp.exp(gcol)
        glast = gcol[:, c_ - 1:c_, :]
        rhs.append(jnp.concatenate([v_ref[h].reshape(nc, c_, hd) * bcol, k[own] * (bcol * eg)],
                                   axis=2))
        k_dec.append((k[own] * jnp.exp(glast - gcol)).astype(BF16))
        q_dec.append(q[own] * eg)
        g_last.append(jnp.exp(glast))

    def chunk_major(per_head):
        stacked = jnp.stack(per_head, axis=1)
        return stacked.reshape((nc * vh,) + stacked.shape[2:])

    pw = chunk_major(neg_a)
    rhs_all = chunk_major(rhs)
    half = (nc * vh) // 2
    n_steps = c_.bit_length() - 1
    bmm = lambda a, b: jnp.einsum("bij,bjk->bik", a, b, preferred_element_type=F32)
    row2 = lax.broadcasted_iota(jnp.int32, (c_, 2 * c_), 0)
    lane2 = lax.broadcasted_iota(jnp.int32, (c_, 2 * c_), 1)
    lane_lo = lane2 < c_
    eye2 = jnp.where((row2 == lane2) | (row2 + c_ == lane2), 1.0, 0.0)

    def block_diag(packed):
        return jnp.concatenate([jnp.where(lane_lo, packed, 0.0), jnp.where(lane_lo, 0.0, packed)],
                               axis=1).astype(BF16)

    q_pow = jnp.concatenate([pw[:half], pw[half:]], axis=2)
    t_inv = eye2 + q_pow
    for s in range(1, n_steps):
        q_pow = bmm(q_pow.astype(BF16), block_diag(q_pow))
        t_inv = t_inv + bmm(q_pow.astype(BF16), block_diag(t_inv))
    stacked_rhs = jnp.concatenate([rhs_all[:half], rhs_all[half:]], axis=1)
    corr = bmm(block_diag(t_inv - eye2), stacked_rhs.astype(BF16))
    sol = rhs_all + jnp.concatenate([corr[:, :c_], corr[:, c_:]], axis=0)
    u = sol[:, :, :hd]
    wq = jnp.concatenate([sol[:, :, hd:], chunk_major(q_dec)], axis=1).astype(BF16)
    qk = chunk_major(qk)
    k_dec = chunk_major(k_dec)
    g_last = chunk_major(g_last)

    nw = nw_ref[...]
    s_cur = [state[h] for h in range(vh)]
    for c in range(nc):
        rows = slice(c * c_, (c + 1) * c_)
        ws = [_dot(wq[c * vh + h], s_cur[h].astype(BF16)) for h in range(vh)]
        v_new = [(u[c * vh + h] - ws[h][:c_]).astype(BF16) for h in range(vh)]
        o = [ws[h][c_:] + _dot(qk[c * vh + h], v_new[h]) for h in range(vh)]
        s_cur = [s_cur[h] * g_last[c * vh + h] + _dot_tn(k_dec[c * vh + h], v_new[h])
                 for h in range(vh)]
        for h in range(vh):
            on = o[h] * lax.rsqrt(jnp.mean(o[h] * o[h], axis=-1, keepdims=True) + NORM_EPS) * nw
            o_ref[rows, h * hd:(h + 1) * hd] = (on * z_ref[h, rows, :]).astype(o_ref.dtype)
    for h in range(vh):
        state[h] = s_cur[h]


def gdn_chunk(proj, col, gct, norm_w, *, kg, tb=512):
    t = proj.shape[1]
    nkh = proj.shape[0] // 6
    vh = 2 * kg
    tb = min(tb, t)
    nc = tb // GDN_CHUNK
    n_chunks = t // GDN_CHUNK
    row = gct.reshape(nkh // kg, vh, n_chunks, 1, GDN_CHUNK)
    return pl.pallas_call(
        functools.partial(_gdn_chunk_kernel, nc=nc, kg=kg),
        grid=(nkh // kg, t // tb),
        in_specs=[
            pl.BlockSpec((kg, tb, HEAD_DIM), lambda g, n: (g, n, 0)),
            pl.BlockSpec((kg, tb, HEAD_DIM), lambda g, n: (nkh // kg + g, n, 0)),
            pl.BlockSpec((vh, tb, HEAD_DIM), lambda g, n: (2 * nkh // vh + g, n, 0)),
            pl.BlockSpec((vh, tb, HEAD_DIM), lambda g, n: (4 * nkh // vh + g, n, 0)),
            pl.BlockSpec((None, tb, 2 * vh), lambda g, n: (g, n, 0)),
            pl.BlockSpec((None, vh, nc, 1, GDN_CHUNK), lambda g, n: (g, 0, n, 0, 0)),
            pl.BlockSpec((1, HEAD_DIM), lambda g, n: (0, 0)),
        ],
        out_specs=pl.BlockSpec((tb, vh * HEAD_DIM), lambda g, n: (n, g)),
        out_shape=jax.ShapeDtypeStruct((t, 2 * nkh * HEAD_DIM), BF16),
        scratch_shapes=[pltpu.VMEM((vh, HEAD_DIM, HEAD_DIM), F32)],
        compiler_params=_params(("arbitrary", "arbitrary"), 48),
        name="gdn_chunk",
    )(proj, proj, proj, proj, col, row, norm_w.reshape(1, HEAD_DIM))


def _top2_of_4(a, b, c, d):
    hi1, lo1 = jnp.maximum(a, b), jnp.minimum(a, b)
    hi2, lo2 = jnp.maximum(c, d), jnp.minimum(c, d)
    return jnp.maximum(hi1, hi2), jnp.maximum(jnp.minimum(hi1, hi2), jnp.maximum(lo1, lo2))


def _first_argmax(vals):
    best, arg = vals[0], jnp.zeros(vals[0].shape, jnp.int32)
    for n in range(1, len(vals)):
        upd = vals[n] > best
        arg = jnp.where(upd, n, arg)
        best = jnp.where(upd, vals[n], best)
    return best, arg


def _router_kernel(x_ref, w_ref, b_ref, idx_ref, gate_ref, rank_ref, cnt_ref, run, *, tm):
    @pl.when(pl.program_id(0) == 0)
    def _():
        run[...] = jnp.zeros(run.shape, F32)

    xh, xl = _split_bf16(x_ref[...])
    wh, wl = _split_bf16(w_ref[...])
    logits = _dot_nt(wh, xh) + _dot_nt(wh, xl) + _dot_nt(wl, xh) + b_ref[...]
    e = jnp.exp(logits - jnp.max(logits, axis=0, keepdims=True))
    p = e / jnp.sum(e, axis=0, keepdims=True)
    rows = [p[r:r + 1, :] for r in range(N_EXPERTS)]
    scores = []
    for g in range(N_GROUPS):
        t1, t2 = _top2_of_4(*rows[g * EXPERTS_PER_GROUP:(g + 1) * EXPERTS_PER_GROUP])
        scores.append(t1 + t2)
    _, grp = _first_argmax(scores)
    vals = []
    for j in range(EXPERTS_PER_GROUP):
        v = rows[(N_GROUPS - 1) * EXPERTS_PER_GROUP + j]
        for g in range(N_GROUPS - 2, -1, -1):
            v = jnp.where(grp == g, rows[g * EXPERTS_PER_GROUP + j], v)
        vals.append(v)
    v1, i1 = _first_argmax(vals)
    v2, i2 = _first_argmax([jnp.where(i1 == j, -1.0, vals[j]) for j in range(EXPERTS_PER_GROUP)])
    e1 = grp * EXPERTS_PER_GROUP + i1
    e2 = grp * EXPERTS_PER_GROUP + i2
    den = v1 + v2
    idx_ref[0:1, :] = e1
    idx_ref[1:2, :] = e2
    gate_ref[0:1, :] = v1 / den
    gate_ref[1:2, :] = v2 / den

    eio = lax.broadcasted_iota(jnp.int32, (N_EXPERTS, tm), 0)
    hit1 = eio == e1
    hit2 = eio == e2
    onehot = jnp.where(hit1 | hit2, 1.0, 0.0)
    r = lax.broadcasted_iota(jnp.int32, (tm, tm), 0)
    c = lax.broadcasted_iota(jnp.int32, (tm, tm), 1)
    before = jnp.where(r < c, 1.0, 0.0).astype(BF16)
    prefix = _dot(onehot.astype(BF16), before) + run[:, 0:1]
    rank_ref[0:1, :] = jnp.sum(jnp.where(hit1, prefix, 0.0), axis=0, keepdims=True).astype(jnp.int32)
    rank_ref[1:2, :] = jnp.sum(jnp.where(hit2, prefix, 0.0), axis=0, keepdims=True).astype(jnp.int32)
    run[...] = run[...] + jnp.sum(onehot, axis=1, keepdims=True)
    cnt_ref[...] = run[...].astype(jnp.int32)


def moe_router(x, router_w, router_b, *, tm=512):
    t, d = x.shape
    tm = min(tm, t)
    out2 = lambda dt: jax.ShapeDtypeStruct((TOP_K, t), dt)
    blk2 = pl.BlockSpec((TOP_K, tm), lambda i: (0, i))
    return pl.pallas_call(
        functools.partial(_router_kernel, tm=tm),
        grid=(t // tm,),
        in_specs=[
            pl.BlockSpec((tm, d), lambda i: (i, 0)),
            pl.BlockSpec((N_EXPERTS, d), lambda i: (0, 0)),
            pl.BlockSpec((N_EXPERTS, 1), lambda i: (0, 0)),
        ],
        out_specs=[blk2, blk2, blk2, pl.BlockSpec((N_EXPERTS, 128), lambda i: (0, 0))],
        out_shape=[out2(jnp.int32), out2(F32), out2(jnp.int32),
                   jax.ShapeDtypeStruct((N_EXPERTS, 128), jnp.int32)],
        scratch_shapes=[pltpu.VMEM((N_EXPERTS, 128), F32)],
        compiler_params=_params(("arbitrary",), 32),
        name="moe_router",
    )(x, router_w.T, router_b.reshape(N_EXPERTS, 1))


def _dispatch_kernel(dest_ref, pend_ref, x_ref, xs_hbm, zbuf, sem, zsem, *, td, t, tm):
    base = pl.program_id(0) * td

    def zero_tile_copy(start):
        return pltpu.make_async_copy(zbuf, xs_hbm.at[pl.ds(pl.multiple_of(start, tm), tm)], zsem)

    @pl.when(pl.program_id(0) == 0)
    def _():
        zbuf[...] = jnp.zeros(zbuf.shape, zbuf.dtype)
        n_rows = xs_hbm.shape[0]
        starts, valid = [], []
        for e in range(N_EXPERTS):
            starts.append(pend_ref[e] - tm)
            valid.append(pend_ref[e] > (pend_ref[e - 1] if e else 0))
            starts.append(pend_ref[N_EXPERTS - 1] + e * tm)
            valid.append(starts[-1] < n_rows)
        for start, ok in zip(starts, valid):
            @pl.when(ok)
            def _():
                zero_tile_copy(start).start()
        for start, ok in zip(starts, valid):
            @pl.when(ok)
            def _():
                zero_tile_copy(start).wait()

    def row_copy(src_row, dst_row):
        return pltpu.make_async_copy(x_ref.at[pl.ds(src_row, 1)], xs_hbm.at[pl.ds(dst_row, 1)], sem)

    def issue(n, carry):
        for k in range(TOP_K):
            row_copy(n, dest_ref[k * t + base + n]).start()
        return carry

    lax.fori_loop(0, td, issue, 0, unroll=DMA_ISSUE_UNROLL)
    all_rows = xs_hbm.at[pl.ds(0, TOP_K * td)]
    pltpu.make_async_copy(all_rows, all_rows, sem).wait()


def moe_dispatch(x, dest_flat, pend, n_rows, *, tm, td=512):
    t, d = x.shape
    td = min(td, t)
    return pl.pallas_call(
        functools.partial(_dispatch_kernel, td=td, t=t, tm=tm),
        grid_spec=pltpu.PrefetchScalarGridSpec(
            num_scalar_prefetch=2,
            grid=(t // td,),
            in_specs=[pl.BlockSpec((td, d), lambda i, dest, pend: (i, 0))],
            out_specs=pl.BlockSpec(memory_space=pl.ANY),
            scratch_shapes=[pltpu.VMEM((tm, d), x.dtype), pltpu.SemaphoreType.DMA(()),
                            pltpu.SemaphoreType.DMA(())],
        ),
        out_shape=jax.ShapeDtypeStruct((n_rows, d), x.dtype),
        compiler_params=_params(("arbitrary",), 32),
        name="moe_dispatch",
    )(dest_flat, pend, x)


def _moe_ffn_kernel(te_ref, first_ref, next_ref, nu_ref, x_ref, wg_hbm, wu_hbm, wd_hbm, y_ref,
                    wg_res, wu_res, wd_res, wg_st, wu_st, wd_st, sem, *, layer, tf):
    i = pl.program_id(0)
    n_chunks = wg_res.shape[1] // tf
    chunk = lambda c: slice(c * tf, (c + 1) * tf)

    def chunk_copies(expert, c):
        slot = c % 2
        return (
            pltpu.make_async_copy(wg_hbm.at[layer, expert, :, chunk(c)], wg_st.at[slot], sem.at[0, slot]),
            pltpu.make_async_copy(wu_hbm.at[layer, expert, :, chunk(c)], wu_st.at[slot], sem.at[1, slot]),
            pltpu.make_async_copy(wd_hbm.at[layer, expert, chunk(c), :], wd_st.at[slot], sem.at[2, slot]),
        )

    def start_chunk(expert, c):
        for copy in chunk_copies(expert, c):
            copy.start()

    def land_chunk(expert, c):
        for copy in chunk_copies(expert, c):
            copy.wait()
        wg_res[:, chunk(c)] = wg_st[c % 2].astype(BF16)
        wu_res[:, chunk(c)] = wu_st[c % 2].astype(BF16)
        wd_res[chunk(c), :] = wd_st[c % 2].astype(BF16)

    def ffn_chunk(x, c):
        h = _silu(_dot(x, wg_res[:, chunk(c)])) * _dot(x, wu_res[:, chunk(c)])
        return _dot(h.astype(BF16), wd_res[chunk(c), :])

    active = i < nu_ref[0]
    is_first = first_ref[i] == 1

    @pl.when(jnp.logical_not(active))
    def _():
        y_ref[...] = jnp.zeros(y_ref.shape, F32)

    @pl.when(i == 0)
    def _():
        for c in range(min(2, n_chunks)):
            start_chunk(te_ref[0], c)

    @pl.when(active & is_first)
    def _():
        expert = te_ref[i]
        x = x_ref[...].astype(BF16)
        acc = None
        for c in range(n_chunks):
            land_chunk(expert, c)
            if c + 2 < n_chunks:
                start_chunk(expert, c + 2)
            part = ffn_chunk(x, c)
            acc = part if acc is None else acc + part
        y_ref[...] = acc

        @pl.when(next_ref[i] >= 0)
        def _():
            for c in range(min(2, n_chunks)):
                start_chunk(next_ref[i], c)

    @pl.when(active & jnp.logical_not(is_first))
    def _():
        x = x_ref[...].astype(BF16)
        acc = None
        for c in range(n_chunks):
            part = ffn_chunk(x, c)
            acc = part if acc is None else acc + part
        y_ref[...] = acc


def moe_ffn(xs, tile_expert, tile_first, tile_next, n_used, wg, wu, wd, layer, *, tm, tf=256):
    n_rows, d = xs.shape
    f_dim = wg.shape[3]
    tf = math.gcd(tf, f_dim)
    row_map = lambda i, te, first, nxt, nu: (jnp.minimum(i, nu[0] - 1), 0)
    any_spec = pl.BlockSpec(memory_space=pl.ANY)
    return pl.pallas_call(
        functools.partial(_moe_ffn_kernel, layer=layer, tf=tf),
        grid_spec=pltpu.PrefetchScalarGridSpec(
            num_scalar_prefetch=4,
            grid=(n_rows // tm,),
            in_specs=[pl.BlockSpec((tm, d), row_map), any_spec, any_spec, any_spec],
            out_specs=pl.BlockSpec((tm, d), lambda i, te, first, nxt, nu: (i, 0)),
            scratch_shapes=[
                pltpu.VMEM((d, f_dim), BF16), pltpu.VMEM((d, f_dim), BF16), pltpu.VMEM((f_dim, d), BF16),
                pltpu.VMEM((2, d, tf), F32), pltpu.VMEM((2, d, tf), F32), pltpu.VMEM((2, tf, d), F32),
                pltpu.SemaphoreType.DMA((3, 2)),
            ],
        ),
        out_shape=jax.ShapeDtypeStruct((n_rows, d), F32),
        compiler_params=_params(("arbitrary",), 56),
        name="moe_ffn",
    )(tile_expert, tile_first, tile_next, n_used, xs, wg, wu, wd)


def _combine_kernel(dest_ref, y_hbm, x_ref, gt_ref, g_ref, b_ref, o_ref, ob_ref, buf, sem,
                    *, tc, n_tiles, t):
    i = pl.program_id(0)

    def row_copy(src_row, slot, k, n):
        return pltpu.make_async_copy(y_hbm.at[pl.ds(src_row, 1)], buf.at[slot, k, pl.ds(n, 1)],
                                     sem.at[slot])

    def issue(tile, slot):
        def body(n, carry):
            for k in range(TOP_K):
                row_copy(dest_ref[k * t + tile * tc + n], slot, k, n).start()
            return carry
        lax.fori_loop(0, tc, body, 0, unroll=DMA_ISSUE_UNROLL)

    @pl.when(i == 0)
    def _():
        issue(0, 0)

    @pl.when(i + 1 < n_tiles)
    def _():
        issue(i + 1, (i + 1) % 2)

    slot = i % 2
    pltpu.make_async_copy(buf.at[slot], buf.at[slot], sem.at[slot]).wait()
    gt = gt_ref[...]
    mix = gt[:, 0:1] * buf[slot, 0] + gt[:, 1:2] * buf[slot, 1]
    out = _layer_norm(DEEPNORM_ALPHA * x_ref[...] + mix, g_ref[...], b_ref[...])
    o_ref[...] = out
    ob_ref[...] = out.astype(BF16)


def moe_combine(y, dest_flat, x, gates_t, ln_g, ln_b, *, tc=256):
    t, d = x.shape
    tc = min(tc, t)
    n_tiles = t // tc
    row = lambda v: v.reshape(1, d)
    blk = lambda i, dest: (i, 0)
    fixed = lambda i, dest: (0, 0)
    return pl.pallas_call(
        functools.partial(_combine_kernel, tc=tc, n_tiles=n_tiles, t=t),
        grid_spec=pltpu.PrefetchScalarGridSpec(
            num_scalar_prefetch=1,
            grid=(n_tiles,),
            in_specs=[
                pl.BlockSpec(memory_space=pl.ANY),
                pl.BlockSpec((tc, d), blk),
                pl.BlockSpec((tc, TOP_K), blk),
                pl.BlockSpec((1, d), fixed),
                pl.BlockSpec((1, d), fixed),
            ],
            out_specs=[pl.BlockSpec((tc, d), blk), pl.BlockSpec((tc, d), blk)],
            scratch_shapes=[pltpu.VMEM((2, TOP_K, tc, d), F32), pltpu.SemaphoreType.DMA((2,))],
        ),
        out_shape=[jax.ShapeDtypeStruct((t, d), F32), jax.ShapeDtypeStruct((t, d), BF16)],
        compiler_params=_params(("arbitrary",), 40),
        name="moe_combine",
    )(dest_flat, y, x, gates_t, row(ln_g), row(ln_b))


def moe_layer(x, router_w, router_b, wg, wu, wd, layer, ln_g, ln_b, *, tm=256):
    t, d = x.shape
    idx, gates, rank, cnt = moe_router(x, router_w, router_b)
    counts = cnt[:, 0]
    padded = (counts + tm - 1) // tm * tm
    pend = jnp.cumsum(padded).astype(jnp.int32)
    pstart = pend - padded
    experts = jnp.arange(N_EXPERTS, dtype=jnp.int32)
    seg_start = jnp.sum(jnp.where(idx[:, :, None] == experts, pstart, 0), axis=-1)
    dest = (seg_start + rank).reshape(TOP_K * t).astype(jnp.int32)
    n_tiles = (t * TOP_K) // tm + N_EXPERTS
    n_used = pend[-1] // tm
    tile_start = jnp.arange(n_tiles, dtype=jnp.int32) * tm
    te = jnp.sum((pend[None, :] <= tile_start[:, None]).astype(jnp.int32), axis=1)
    last_used = jnp.sum((pend <= (n_used - 1) * tm).astype(jnp.int32))
    te = jnp.where(tile_start < pend[-1], te, last_used).astype(jnp.int32)
    first = jnp.concatenate([jnp.ones((1,), jnp.int32), (te[1:] != te[:-1]).astype(jnp.int32)])
    later_nonempty = (counts[None, :] > 0) & (experts[None, :] > experts[:, None])
    next_expert = jnp.min(jnp.where(later_nonempty, experts[None, :], N_EXPERTS), axis=1)
    next_expert = jnp.where(next_expert < N_EXPERTS, next_expert, -1)
    tile_next = jnp.sum(jnp.where(te[:, None] == experts, next_expert, 0), axis=1).astype(jnp.int32)

    xs = moe_dispatch(x, dest, pend, n_tiles * tm, tm=tm)
    y = moe_ffn(xs, te, first, tile_next, n_used.reshape(1), wg, wu, wd, layer, tm=tm)
    return moe_combine(y, dest, x, gates.T, ln_g, ln_b)


def kernel(x, conv_w_pw1, conv_b_pw1, conv_w_dw, conv_b_dw, conv_ln_g, conv_ln_b, conv_w_pw2,
           conv_b_pw2, gdn_w_in, gdn_w_conv, gdn_a_log, gdn_dt_bias, gdn_norm_w, gdn_w_out,
           router_w, router_b, moe_w_gate, moe_w_up, moe_w_down, ln_mix_g, ln_mix_b, ln_ffn_g,
           ln_ffn_b):
    b_, s_, d = x.shape
    t = b_ * s_
    xf = x.reshape(t, d)
    xb = xf.astype(BF16)
    n_main = 6 * d
    n_vh = gdn_a_log.shape[1]
    kg = min(GDN_KEY_HEADS_PER_STEP, n_vh // 2)
    for i in range(DEPTH):
        j = i // 2
        if i % 2 == 0:
            h = pw1_glu(xb, conv_w_pw1, conv_b_pw1[j], j)
            hb = dwconv_ln_silu(h, conv_w_dw[j], conv_b_dw[j], conv_ln_g[j], conv_ln_b[j])
            xf, xb = mm_res_ln(hb, conv_w_pw2, j, conv_b_pw2[j], xf, ln_mix_g[i], ln_mix_b[i])
        else:
            proj = gdn_in_proj(xb, gdn_w_in, j, gdn_w_conv[j])
            col, gct = gdn_gates(xf, gdn_w_in[j, :, n_main:n_main + n_vh],
                                 gdn_w_in[j, :, n_main + n_vh:], gdn_a_log[j], gdn_dt_bias[j],
                                 hg=2 * kg)
            ob = gdn_chunk(proj, col, gct, gdn_norm_w[j], kg=kg)
            xf, xb = mm_res_ln(ob, gdn_w_out, j, jnp.zeros((d,), F32), xf, ln_mix_g[i], ln_mix_b[i])
        xf, xb = moe_layer(xf, router_w, router_b, moe_w_gate, moe_w_up, moe_w_down, i,
                           ln_ffn_g[i], ln_ffn_b[i])
    return xf.reshape(b_, s_, d)
```

```python
import functools
import math

import jax
import jax.numpy as jnp
from jax import lax
from jax.experimental import pallas as pl
from jax.experimental.pallas import tpu as pltpu

F32 = jnp.float32
BF16 = jnp.bfloat16

DEPTH = 4
DEEPNORM_ALPHA = (2 * DEPTH) ** 0.25
LN_EPS = 1e-5
NORM_EPS = 1e-6
HEAD_DIM = 128
GDN_CHUNK = 64
N_EXPERTS = 16
N_GROUPS = 4
EXPERTS_PER_GROUP = N_EXPERTS // N_GROUPS
TOP_K = 2
SUBLANES = 8
HALO_ROWS = 32
CARRY_ROWS = 8
DMA_ISSUE_UNROLL = 8
GDN_KEY_HEADS_PER_STEP = 4
MIB = 1024 * 1024


def _params(semantics, vmem_mib):
    return pltpu.CompilerParams(dimension_semantics=semantics, vmem_limit_bytes=vmem_mib * MIB)


def _sigmoid(x):
    return 0.5 * jnp.tanh(0.5 * x) + 0.5


def _silu(x):
    return x * _sigmoid(x)


def _layer_norm(y, g, b):
    mu = jnp.mean(y, axis=-1, keepdims=True)
    d = y - mu
    var = jnp.mean(d * d, axis=-1, keepdims=True)
    return d * lax.rsqrt(var + LN_EPS) * g + b


def _split_bf16(x):
    hi = x.astype(BF16)
    lo = (x - hi.astype(F32)).astype(BF16)
    return hi, lo


def _dot(a, b):
    return jnp.dot(a, b, preferred_element_type=F32)


def _dot_nt(a, b):
    return lax.dot_general(a, b, (((1,), (1,)), ((), ())), preferred_element_type=F32)


def _dot_tn(a, b):
    return lax.dot_general(a, b, (((0,), (0,)), ((), ())), preferred_element_type=F32)


def _dot3(x, w):
    xh, xl = _split_bf16(x)
    wh, wl = _split_bf16(w)
    return _dot(xh, wh) + _dot(xl, wh) + _dot(xh, wl)


def _pw1_glu_kernel(x_ref, wv_ref, wg_ref, bv_ref, bg_ref, o_ref, wv_s, wg_s):
    @pl.when(pl.program_id(1) == 0)
    def _():
        wv_s[...] = wv_ref[...].astype(BF16)
        wg_s[...] = wg_ref[...].astype(BF16)

    x = x_ref[...]
    val = _dot(x, wv_s[...]) + bv_ref[...]
    gate = _dot(x, wg_s[...]) + bg_ref[...]
    o_ref[...] = val * _sigmoid(gate)


def pw1_glu(xb, w_all, b, layer, *, tm=1024, tn=512):
    m, k = xb.shape
    n = w_all.shape[2] // 2
    tm, tn = min(tm, m), min(tn, n)
    nj = n // tn
    b2 = b.reshape(1, 2 * n)
    w = w_all
    return pl.pallas_call(
        _pw1_glu_kernel,
        grid=(nj, m // tm),
        in_specs=[
            pl.BlockSpec((tm, k), lambda j, i: (i, 0)),
            pl.BlockSpec((None, k, tn), lambda j, i: (layer, 0, j)),
            pl.BlockSpec((None, k, tn), lambda j, i: (layer, 0, j + nj)),
            pl.BlockSpec((1, tn), lambda j, i: (0, j)),
            pl.BlockSpec((1, tn), lambda j, i: (0, j + nj)),
        ],
        out_specs=pl.BlockSpec((tm, tn), lambda j, i: (i, j)),
        out_shape=jax.ShapeDtypeStruct((m, n), F32),
        scratch_shapes=[pltpu.VMEM((k, tn), BF16), pltpu.VMEM((k, tn), BF16)],
        compiler_params=_params(("arbitrary", "arbitrary"), 48),
        name="pw1_glu",
    )(xb, w, w, b2, b2)


def _dwconv_ln_silu_kernel(h_ref, halo_ref, w_ref, b_ref, g_ref, beta_ref, o_ref, buf, cbuf,
                           *, ts, width, cw, rb):
    i = pl.program_id(0)
    d = h_ref.shape[1]
    buf[0:HALO_ROWS, :] = jnp.where(i > 0, halo_ref[...], 0.0)
    buf[HALO_ROWS:HALO_ROWS + ts, :] = h_ref[...]

    def col_body(c, carry):
        cols = pl.ds(pl.multiple_of(c * cw, cw), cw)
        for r in range(ts // rb):
            acc = jnp.broadcast_to(b_ref[:, cols], (rb, cw))
            for b in range(SUBLANES):
                yb = None
                for a in range((width - 1 - b) // SUBLANES + 1):
                    j = width - 1 - (SUBLANES * a + b)
                    start = r * rb + HALO_ROWS - SUBLANES * (a + 1)
                    term = w_ref[j:j + 1, cols] * buf[pl.ds(start, rb + SUBLANES), cols]
                    yb = term if yb is None else yb + term
                acc = acc + yb[SUBLANES - b:SUBLANES - b + rb]
            cbuf[pl.ds(r * rb, rb), cols] = acc
        return carry

    lax.fori_loop(0, d // cw, col_body, 0)
    y = _layer_norm(cbuf[...], g_ref[...], beta_ref[...])
    o_ref[...] = _silu(y).astype(o_ref.dtype)


def dwconv_ln_silu(h, w_dw, b_dw, ln_g, ln_b, *, ts=256, cw=256, rb=64):
    t, d = h.shape
    width = w_dw.shape[0]
    assert SUBLANES * ((width - 1) // SUBLANES + 1) <= HALO_ROWS
    ts, cw = min(ts, t), min(cw, d)
    halo_per_tile = ts // HALO_ROWS
    kern = functools.partial(_dwconv_ln_silu_kernel, ts=ts, width=width, cw=cw, rb=rb)
    row = lambda a: a.reshape(1, d)
    return pl.pallas_call(
        kern,
        grid=(t // ts,),
        in_specs=[
            pl.BlockSpec((ts, d), lambda i: (i, 0)),
            pl.BlockSpec((HALO_ROWS, d), lambda i: (jnp.maximum(i * halo_per_tile - 1, 0), 0)),
            pl.BlockSpec((width, d), lambda i: (0, 0)),
            pl.BlockSpec((1, d), lambda i: (0, 0)),
            pl.BlockSpec((1, d), lambda i: (0, 0)),
            pl.BlockSpec((1, d), lambda i: (0, 0)),
        ],
        out_specs=pl.BlockSpec((ts, d), lambda i: (i, 0)),
        out_shape=jax.ShapeDtypeStruct((t, d), BF16),
        scratch_shapes=[pltpu.VMEM((ts + HALO_ROWS, d), F32), pltpu.VMEM((ts, d), F32)],
        compiler_params=_params(("arbitrary",), 40),
        name="dwconv_ln_silu",
    )(h, h, w_dw, row(b_dw), row(ln_g), row(ln_b))


def _mm_res_ln_kernel(a_ref, w_hbm, bias_ref, res_ref, g_ref, b_ref, o_ref, ob_ref, wres, stage,
                      sem, *, layer, tk, sub):
    nk = wres.shape[0] // tk

    @pl.when(pl.program_id(0) == 0)
    def _():
        def chunk_copy(c):
            return pltpu.make_async_copy(w_hbm.at[layer, pl.ds(c * tk, tk)], stage.at[c % 2],
                                         sem.at[c % 2])
        chunk_copy(0).start()
        for c in range(nk):
            if c + 1 < nk:
                chunk_copy(c + 1).start()
            chunk_copy(c).wait()
            wres[c * tk:(c + 1) * tk, :] = stage[c % 2].astype(BF16)

    for sb in range(a_ref.shape[0] // sub):
        rows = slice(sb * sub, (sb + 1) * sub)
        y = DEEPNORM_ALPHA * res_ref[rows, :] + (_dot(a_ref[rows, :], wres[...]) + bias_ref[...])
        out = _layer_norm(y, g_ref[...], b_ref[...])
        o_ref[rows, :] = out
        ob_ref[rows, :] = out.astype(BF16)


def mm_res_ln(a, w_all, layer, bias, res, ln_g, ln_b, *, tm=256, tk=512, sub=128):
    m, k = a.shape
    n = w_all.shape[2]
    tm, tk = min(tm, m), min(tk, k)
    row = lambda v: v.reshape(1, n)
    return pl.pallas_call(
        functools.partial(_mm_res_ln_kernel, layer=layer, tk=tk, sub=min(sub, tm)),
        grid=(m // tm,),
        in_specs=[
            pl.BlockSpec((tm, k), lambda i: (i, 0)),
            pl.BlockSpec(memory_space=pl.ANY),
            pl.BlockSpec((1, n), lambda i: (0, 0)),
            pl.BlockSpec((tm, n), lambda i: (i, 0)),
            pl.BlockSpec((1, n), lambda i: (0, 0)),
            pl.BlockSpec((1, n), lambda i: (0, 0)),
        ],
        out_specs=[pl.BlockSpec((tm, n), lambda i: (i, 0)),
                   pl.BlockSpec((tm, n), lambda i: (i, 0))],
        out_shape=[jax.ShapeDtypeStruct((m, n), F32), jax.ShapeDtypeStruct((m, n), BF16)],
        scratch_shapes=[pltpu.VMEM((k, n), BF16), pltpu.VMEM((2, tk, n), F32),
                        pltpu.SemaphoreType.DMA((2,))],
        compiler_params=_params(("arbitrary",), 48),
        name="mm_res_ln",
    )(a, w_all, row(bias), res, row(ln_g), row(ln_b))


def _gdn_in_kernel(x_ref, w_ref, wc_ref, o_ref, wbf, tail_ref, *, tm, tn, sub, nq, nqk, nconv, cwidth):
    j = pl.program_id(0)
    i = pl.program_id(1)
    heads = [slice(hh * HEAD_DIM, (hh + 1) * HEAD_DIM) for hh in range(tn // HEAD_DIM)]

    @pl.when(i == 0)
    def _():
        wbf[...] = w_ref[...].astype(BF16)
        tail_ref[...] = jnp.zeros((CARRY_ROWS, tn), F32)

    def for_sub_blocks(epilogue, carry_tail):
        tail = tail_ref[...] if carry_tail else None
        n_sub = tm // sub
        sub_rows = [slice(sb * sub, (sb + 1) * sub) for sb in range(n_sub)]
        p_next = _dot(x_ref[sub_rows[0], :], wbf[...])
        for sb in range(n_sub):
            p = p_next
            if sb + 1 < n_sub:
                p_next = _dot(x_ref[sub_rows[sb + 1], :], wbf[...])
            epilogue(sub_rows[sb], p, tail)
            tail = p[sub - CARRY_ROWS:, :] if carry_tail else None
        if carry_tail:
            tail_ref[...] = tail

    def conv_silu(p, tail):
        ext = jnp.concatenate([tail, p], axis=0)
        c = wc_ref[cwidth - 1:cwidth, :] * p
        for jj in range(cwidth - 1):
            off = CARRY_ROWS - (cwidth - 1) + jj
            c = c + wc_ref[jj:jj + 1, :] * ext[off:off + sub, :]
        return _silu(c)

    def store_heads(rows, s):
        for hh, cols in enumerate(heads):
            o_ref[hh, rows, :] = s[:, cols]

    def gate_epilogue(rows, p, tail):
        store_heads(rows, _silu(p))

    def value_epilogue(rows, p, tail):
        store_heads(rows, conv_silu(p, tail))

    def qk_epilogue(rows, p, tail):
        s = conv_silu(p, tail)
        scale = jnp.where(j < nq, HEAD_DIM ** -0.5, 1.0).astype(F32)
        for hh, cols in enumerate(heads):
            blk = s[:, cols]
            ss = jnp.sum(blk * blk, axis=-1, keepdims=True)
            o_ref[hh, rows, :] = blk * lax.rsqrt(ss + NORM_EPS) * scale

    @pl.when(j >= nconv)
    def _():
        for_sub_blocks(gate_epilogue, False)

    @pl.when((j >= nqk) & (j < nconv))
    def _():
        for_sub_blocks(value_epilogue, True)

    @pl.when(j < nqk)
    def _():
        for_sub_blocks(qk_epilogue, True)


def gdn_in_proj(xb, w_all, layer, w_conv, *, tm=1024, tn=512, sub=128):
    t, d = xb.shape
    qk_dim = d
    v_dim = 2 * d
    n_main = 2 * qk_dim + 2 * v_dim
    tm, tn = min(tm, t), min(tn, qk_dim)
    cwidth = w_conv.shape[0]
    nq = qk_dim // tn
    nconv = (2 * qk_dim + v_dim) // tn
    hpt = tn // HEAD_DIM
    kern = functools.partial(_gdn_in_kernel, tm=tm, tn=tn, sub=min(sub, tm), nq=nq, nqk=2 * nq,
                             nconv=nconv, cwidth=cwidth)
    return pl.pallas_call(
        kern,
        grid=(n_main // tn, t // tm),
        in_specs=[
            pl.BlockSpec((tm, d), lambda j, i: (i, 0)),
            pl.BlockSpec((None, d, tn), lambda j, i: (layer, 0, j)),
            pl.BlockSpec((cwidth, tn), lambda j, i: (0, jnp.minimum(j, nconv - 1))),
        ],
        out_specs=pl.BlockSpec((hpt, tm, HEAD_DIM), lambda j, i: (j, i, 0)),
        out_shape=jax.ShapeDtypeStruct((n_main // HEAD_DIM, t, HEAD_DIM), F32),
        scratch_shapes=[pltpu.VMEM((d, tn), BF16), pltpu.VMEM((CARRY_ROWS, tn), F32)],
        compiler_params=_params(("arbitrary", "arbitrary"), 48),
        name="gdn_in_proj",
    )(xb, w_all, w_conv)


def _gdn_gate_kernel(x_ref, wb_ref, wa_ref, alog_ref, dtb_ref, col_ref, gct_ref, *, tm, hg):
    x = x_ref[...]
    beta = _sigmoid(_dot3(x, wb_ref[...]))
    a = _dot3(x, wa_ref[...]) + dtb_ref[...]
    softplus = jnp.maximum(a, 0.0) + jnp.log(1.0 + jnp.exp(-jnp.abs(a)))
    g = -jnp.exp(alog_ref[...]) * softplus
    r = lax.broadcasted_iota(jnp.int32, (tm, tm), 0)
    c = lax.broadcasted_iota(jnp.int32, (tm, tm), 1)
    same = r // GDN_CHUNK == c // GDN_CHUNK
    lower = jnp.where(same & (c <= r), 1.0, 0.0).astype(BF16)
    upper = jnp.where(same & (r <= c), 1.0, 0.0).astype(BF16)
    g1 = g.astype(BF16)
    r1 = g - g1.astype(F32)
    g2 = r1.astype(BF16)
    g3 = (r1 - g2.astype(F32)).astype(BF16)
    gc = _dot(lower, g1) + _dot(lower, g2) + _dot(lower, g3)
    gct_ref[...] = _dot_tn(g1, upper) + _dot_tn(g2, upper) + _dot_tn(g3, upper)
    for grp in range(col_ref.shape[0]):
        cols = slice(grp * hg, (grp + 1) * hg)
        col_ref[grp] = jnp.concatenate([beta[:, cols], gc[:, cols]], axis=1)


def gdn_gates(x, w_beta, w_a, a_log, dt_bias, *, hg, tm=512):
    t, d = x.shape
    h = w_beta.shape[1]
    tm = min(tm, t)
    row = lambda v: v.reshape(1, h)
    return pl.pallas_call(
        functools.partial(_gdn_gate_kernel, tm=tm, hg=hg),
        grid=(t // tm,),
        in_specs=[
            pl.BlockSpec((tm, d), lambda i: (i, 0)),
            pl.BlockSpec((d, h), lambda i: (0, 0)),
            pl.BlockSpec((d, h), lambda i: (0, 0)),
            pl.BlockSpec((1, h), lambda i: (0, 0)),
            pl.BlockSpec((1, h), lambda i: (0, 0)),
        ],
        out_specs=[pl.BlockSpec((h // hg, tm, 2 * hg), lambda i: (0, i, 0)),
                   pl.BlockSpec((h, tm), lambda i: (0, i))],
        out_shape=[jax.ShapeDtypeStruct((h // hg, t, 2 * hg), F32),
                   jax.ShapeDtypeStruct((h, t), F32)],
        compiler_params=_params(("arbitrary",), 32),
        name="gdn_gates",
    )(x, w_beta, w_a, row(a_log), row(dt_bias))


def _gdn_chunk_kernel(q_ref, k_ref, v_ref, z_ref, col_ref, row_ref, nw_ref, o_ref, state,
                      *, nc, kg):
    c_ = GDN_CHUNK
    hd = HEAD_DIM
    vh = 2 * kg

    @pl.when(pl.program_id(1) == 0)
    def _():
        state[...] = jnp.zeros(state.shape, F32)

    ii = lax.broadcasted_iota(jnp.int32, (c_, c_), 0)
    jj = lax.broadcasted_iota(jnp.int32, (c_, c_), 1)
    tril = ii >= jj
    strict = ii > jj

    q = q_ref[...].reshape(kg * nc, c_, hd)
    k = k_ref[...].reshape(kg * nc, c_, hd)
    kq = lax.dot_general(jnp.concatenate([k, q], axis=1).astype(BF16), k.astype(BF16),
                         (((2,), (2,)), ((0,), (0,))), preferred_element_type=F32)
    col = col_ref[...]
    neg_a, rhs, qk, k_dec, q_dec, g_last = [], [], [], [], [], []
    for h in range(vh):
        own = slice((h // 2) * nc, (h // 2 + 1) * nc)
        bcol = jnp.broadcast_to(col[:, h:h + 1], (nc * c_, hd)).reshape(nc, c_, hd)
        gcol = jnp.broadcast_to(col[:, vh + h:vh + h + 1], (nc * c_, hd)).reshape(nc, c_, hd)
        grow = row_ref[h]
        decay = jnp.where(tril, jnp.exp(jnp.minimum(gcol[:, :, :c_] - grow, 0.0)), 0.0)
        neg_a.append(jnp.where(strict, -(bcol[:, :, :c_] * kq[own, :c_] * decay), 0.0))
        qk.append((kq[own, c_:] * decay).astype(BF16))
        eg = jnp.exp(gcol)
        glast = gcol[:, c_ - 1:c_, :]
        rhs.append(jnp.concatenate([v_ref[h].reshape(nc, c_, hd) * bcol, k[own] * (bcol * eg)],
                                   axis=2))
        k_dec.append((k[own] * jnp.exp(glast - gcol)).astype(BF16))
        q_dec.append(q[own] * eg)
        g_last.append(jnp.exp(glast))

    def chunk_major(per_head):
        stacked = jnp.stack(per_head, axis=1)
        return stacked.reshape((nc * vh,) + stacked.shape[2:])

    pw = chunk_major(neg_a)
    rhs_all = chunk_major(rhs)
    half = (nc * vh) // 2
    n_steps = c_.bit_length() - 1
    bmm = lambda a, b: jnp.einsum("bij,bjk->bik", a, b, preferred_element_type=F32)
    row2 = lax.broadcasted_iota(jnp.int32, (c_, 2 * c_), 0)
    lane2 = lax.broadcasted_iota(jnp.int32, (c_, 2 * c_), 1)
    lane_lo = lane2 < c_
    eye2 = jnp.where((row2 == lane2) | (row2 + c_ == lane2), 1.0, 0.0)

    def block_diag(packed):
        return jnp.concatenate([jnp.where(lane_lo, packed, 0.0), jnp.where(lane_lo, 0.0, packed)],
                               axis=1).astype(BF16)

    q_pow = jnp.concatenate([pw[:half], pw[half:]], axis=2)
    t_inv = eye2 + q_pow
    for s in range(1, n_steps):
        q_pow = bmm(q_pow.astype(BF16), block_diag(q_pow))
        t_inv = t_inv + bmm(q_pow.astype(BF16), block_diag(t_inv))
    stacked_rhs = jnp.concatenate([rhs_all[:half], rhs_all[half:]], axis=1)
    corr = bmm(block_diag(t_inv - eye2), stacked_rhs.astype(BF16))
    sol = rhs_all + jnp.concatenate([corr[:, :c_], corr[:, c_:]], axis=0)
    u = sol[:, :, :hd]
    wq = jnp.concatenate([sol[:, :, hd:], chunk_major(q_dec)], axis=1).astype(BF16)
    qk = chunk_major(qk)
    k_dec = chunk_major(k_dec)
    g_last = chunk_major(g_last)

    nw = nw_ref[...]
    s_cur = [state[h] for h in range(vh)]
    for c in range(nc):
        rows = slice(c * c_, (c + 1) * c_)
        ws = [_dot(wq[c * vh + h], s_cur[h].astype(BF16)) for h in range(vh)]
        v_new = [(u[c * vh + h] - ws[h][:c_]).astype(BF16) for h in range(vh)]
        o = [ws[h][c_:] + _dot(qk[c * vh + h], v_new[h]) for h in range(vh)]
        s_cur = [s_cur[h] * g_last[c * vh + h] + _dot_tn(k_dec[c * vh + h], v_new[h])
                 for h in range(vh)]
        for h in range(vh):
            on = o[h] * lax.rsqrt(jnp.mean(o[h] * o[h], axis=-1, keepdims=True) + NORM_EPS) * nw
            o_ref[rows, h * hd:(h + 1) * hd] = (on * z_ref[h, rows, :]).astype(o_ref.dtype)
    for h in range(vh):
        state[h] = s_cur[h]


def gdn_chunk(proj, col, gct, norm_w, *, kg, tb=512):
    t = proj.shape[1]
    nkh = proj.shape[0] // 6
    vh = 2 * kg
    tb = min(tb, t)
    nc = tb // GDN_CHUNK
    n_chunks = t // GDN_CHUNK
    row = gct.reshape(nkh // kg, vh, n_chunks, 1, GDN_CHUNK)
    return pl.pallas_call(
        functools.partial(_gdn_chunk_kernel, nc=nc, kg=kg),
        grid=(nkh // kg, t // tb),
        in_specs=[
            pl.BlockSpec((kg, tb, HEAD_DIM), lambda g, n: (g, n, 0)),
            pl.BlockSpec((kg, tb, HEAD_DIM), lambda g, n: (nkh // kg + g, n, 0)),
            pl.BlockSpec((vh, tb, HEAD_DIM), lambda g, n: (2 * nkh // vh + g, n, 0)),
            pl.BlockSpec((vh, tb, HEAD_DIM), lambda g, n: (4 * nkh // vh + g, n, 0)),
            pl.BlockSpec((None, tb, 2 * vh), lambda g, n: (g, n, 0)),
            pl.BlockSpec((None, vh, nc, 1, GDN_CHUNK), lambda g, n: (g, 0, n, 0, 0)),
            pl.BlockSpec((1, HEAD_DIM), lambda g, n: (0, 0)),
        ],
        out_specs=pl.BlockSpec((tb, vh * HEAD_DIM), lambda g, n: (n, g)),
        out_shape=jax.ShapeDtypeStruct((t, 2 * nkh * HEAD_DIM), BF16),
        scratch_shapes=[pltpu.VMEM((vh, HEAD_DIM, HEAD_DIM), F32)],
        compiler_params=_params(("arbitrary", "arbitrary"), 48),
        name="gdn_chunk",
    )(proj, proj, proj, proj, col, row, norm_w.reshape(1, HEAD_DIM))


def _top2_of_4(a, b, c, d):
    hi1, lo1 = jnp.maximum(a, b), jnp.minimum(a, b)
    hi2, lo2 = jnp.maximum(c, d), jnp.minimum(c, d)
    return jnp.maximum(hi1, hi2), jnp.maximum(jnp.minimum(hi1, hi2), jnp.maximum(lo1, lo2))


def _first_argmax(vals):
    best, arg = vals[0], jnp.zeros(vals[0].shape, jnp.int32)
    for n in range(1, len(vals)):
        upd = vals[n] > best
        arg = jnp.where(upd, n, arg)
        best = jnp.where(upd, vals[n], best)
    return best, arg


def _router_kernel(x_ref, w_ref, b_ref, idx_ref, gate_ref, rank_ref, cnt_ref, run, *, tm):
    @pl.when(pl.program_id(0) == 0)
    def _():
        run[...] = jnp.zeros(run.shape, F32)

    xh, xl = _split_bf16(x_ref[...])
    wh, wl = _split_bf16(w_ref[...])
    logits = _dot_nt(wh, xh) + _dot_nt(wh, xl) + _dot_nt(wl, xh) + b_ref[...]
    e = jnp.exp(logits - jnp.max(logits, axis=0, keepdims=True))
    p = e / jnp.sum(e, axis=0, keepdims=True)
    rows = [p[r:r + 1, :] for r in range(N_EXPERTS)]
    scores = []
    for g in range(N_GROUPS):
        t1, t2 = _top2_of_4(*rows[g * EXPERTS_PER_GROUP:(g + 1) * EXPERTS_PER_GROUP])
        scores.append(t1 + t2)
    _, grp = _first_argmax(scores)
    vals = []
    for j in range(EXPERTS_PER_GROUP):
        v = rows[(N_GROUPS - 1) * EXPERTS_PER_GROUP + j]
        for g in range(N_GROUPS - 2, -1, -1):
            v = jnp.where(grp == g, rows[g * EXPERTS_PER_GROUP + j], v)
        vals.append(v)
    v1, i1 = _first_argmax(vals)
    v2, i2 = _first_argmax([jnp.where(i1 == j, -1.0, vals[j]) for j in range(EXPERTS_PER_GROUP)])
    e1 = grp * EXPERTS_PER_GROUP + i1
    e2 = grp * EXPERTS_PER_GROUP + i2
    den = v1 + v2
    idx_ref[0:1, :] = e1
    idx_ref[1:2, :] = e2
    gate_ref[0:1, :] = v1 / den
    gate_ref[1:2, :] = v2 / den

    eio = lax.broadcasted_iota(jnp.int32, (N_EXPERTS, tm), 0)
    hit1 = eio == e1
    hit2 = eio == e2
    onehot = jnp.where(hit1 | hit2, 1.0, 0.0)
    r = lax.broadcasted_iota(jnp.int32, (tm, tm), 0)
    c = lax.broadcasted_iota(jnp.int32, (tm, tm), 1)
    before = jnp.where(r < c, 1.0, 0.0).astype(BF16)
    prefix = _dot(onehot.astype(BF16), before) + run[:, 0:1]
    rank_ref[0:1, :] = jnp.sum(jnp.where(hit1, prefix, 0.0), axis=0, keepdims=True).astype(jnp.int32)
    rank_ref[1:2, :] = jnp.sum(jnp.where(hit2, prefix, 0.0), axis=0, keepdims=True).astype(jnp.int32)
    run[...] = run[...] + jnp.sum(onehot, axis=1, keepdims=True)
    cnt_ref[...] = run[...].astype(jnp.int32)


def moe_router(x, router_w, router_b, *, tm=512):
    t, d = x.shape
    tm = min(tm, t)
    out2 = lambda dt: jax.ShapeDtypeStruct((TOP_K, t), dt)
    blk2 = pl.BlockSpec((TOP_K, tm), lambda i: (0, i))
    return pl.pallas_call(
        functools.partial(_router_kernel, tm=tm),
        grid=(t // tm,),
        in_specs=[
            pl.BlockSpec((tm, d), lambda i: (i, 0)),
            pl.BlockSpec((N_EXPERTS, d), lambda i: (0, 0)),
            pl.BlockSpec((N_EXPERTS, 1), lambda i: (0, 0)),
        ],
        out_specs=[blk2, blk2, blk2, pl.BlockSpec((N_EXPERTS, 128), lambda i: (0, 0))],
        out_shape=[out2(jnp.int32), out2(F32), out2(jnp.int32),
                   jax.ShapeDtypeStruct((N_EXPERTS, 128), jnp.int32)],
        scratch_shapes=[pltpu.VMEM((N_EXPERTS, 128), F32)],
        compiler_params=_params(("arbitrary",), 32),
        name="moe_router",
    )(x, router_w.T, router_b.reshape(N_EXPERTS, 1))


def _dispatch_kernel(dest_ref, pend_ref, x_ref, xs_hbm, zbuf, sem, zsem, *, td, t, tm):
    base = pl.program_id(0) * td

    def zero_tile_copy(start):
        return pltpu.make_async_copy(zbuf, xs_hbm.at[pl.ds(pl.multiple_of(start, tm), tm)], zsem)

    @pl.when(pl.program_id(0) == 0)
    def _():
        zbuf[...] = jnp.zeros(zbuf.shape, zbuf.dtype)
        n_rows = xs_hbm.shape[0]
        starts, valid = [], []
        for e in range(N_EXPERTS):
            starts.append(pend_ref[e] - tm)
            valid.append(pend_ref[e] > (pend_ref[e - 1] if e else 0))
            starts.append(pend_ref[N_EXPERTS - 1] + e * tm)
            valid.append(starts[-1] < n_rows)
        for start, ok in zip(starts, valid):
            @pl.when(ok)
            def _():
                zero_tile_copy(start).start()
        for start, ok in zip(starts, valid):
            @pl.when(ok)
            def _():
                zero_tile_copy(start).wait()

    def row_copy(src_row, dst_row):
        return pltpu.make_async_copy(x_ref.at[pl.ds(src_row, 1)], xs_hbm.at[pl.ds(dst_row, 1)], sem)

    def issue(n, carry):
        for k in range(TOP_K):
            row_copy(n, dest_ref[k * t + base + n]).start(priority=k % 2)
        return carry

    lax.fori_loop(0, td, issue, 0, unroll=DMA_ISSUE_UNROLL)
    all_rows = xs_hbm.at[pl.ds(0, TOP_K * td)]
    pltpu.make_async_copy(all_rows, all_rows, sem).wait()


def moe_dispatch(x, dest_flat, pend, n_rows, *, tm, td=512):
    t, d = x.shape
    td = min(td, t)
    return pl.pallas_call(
        functools.partial(_dispatch_kernel, td=td, t=t, tm=tm),
        grid_spec=pltpu.PrefetchScalarGridSpec(
            num_scalar_prefetch=2,
            grid=(t // td,),
            in_specs=[pl.BlockSpec((td, d), lambda i, dest, pend: (i, 0))],
            out_specs=pl.BlockSpec(memory_space=pl.ANY),
            scratch_shapes=[pltpu.VMEM((tm, d), x.dtype), pltpu.SemaphoreType.DMA(()),
                            pltpu.SemaphoreType.DMA(())],
        ),
        out_shape=jax.ShapeDtypeStruct((n_rows, d), x.dtype),
        compiler_params=_params(("arbitrary",), 32),
        name="moe_dispatch",
    )(dest_flat, pend, x)


def _moe_ffn_kernel(te_ref, first_ref, next_ref, nu_ref, x_ref, wg_hbm, wu_hbm, wd_hbm, y_ref,
                    wg_res, wu_res, wd_res, wg_st, wu_st, wd_st, sem, *, layer, tf):
    i = pl.program_id(0)
    n_chunks = wg_res.shape[1] // tf
    chunk = lambda c: slice(c * tf, (c + 1) * tf)

    def chunk_copies(expert, c):
        slot = c % 2
        return (
            pltpu.make_async_copy(wg_hbm.at[layer, expert, :, chunk(c)], wg_st.at[slot], sem.at[0, slot]),
            pltpu.make_async_copy(wu_hbm.at[layer, expert, :, chunk(c)], wu_st.at[slot], sem.at[1, slot]),
            pltpu.make_async_copy(wd_hbm.at[layer, expert, chunk(c), :], wd_st.at[slot], sem.at[2, slot]),
        )

    def start_chunk(expert, c):
        for copy in chunk_copies(expert, c):
            copy.start()

    def land_chunk(expert, c):
        for copy in chunk_copies(expert, c):
            copy.wait()
        wg_res[:, chunk(c)] = wg_st[c % 2].astype(BF16)
        wu_res[:, chunk(c)] = wu_st[c % 2].astype(BF16)
        wd_res[chunk(c), :] = wd_st[c % 2].astype(BF16)

    def ffn_chunk(x, c):
        h = _silu(_dot(x, wg_res[:, chunk(c)])) * _dot(x, wu_res[:, chunk(c)])
        return _dot(h.astype(BF16), wd_res[chunk(c), :])

    active = i < nu_ref[0]
    is_first = first_ref[i] == 1

    @pl.when(jnp.logical_not(active))
    def _():
        y_ref[...] = jnp.zeros(y_ref.shape, F32)

    @pl.when(i == 0)
    def _():
        for c in range(min(2, n_chunks)):
            start_chunk(te_ref[0], c)

    @pl.when(active & is_first)
    def _():
        expert = te_ref[i]
        x = x_ref[...].astype(BF16)
        acc = None
        for c in range(n_chunks):
            land_chunk(expert, c)
            if c + 2 < n_chunks:
                start_chunk(expert, c + 2)
            part = ffn_chunk(x, c)
            acc = part if acc is None else acc + part
        y_ref[...] = acc

        @pl.when(next_ref[i] >= 0)
        def _():
            for c in range(min(2, n_chunks)):
                start_chunk(next_ref[i], c)

    @pl.when(active & jnp.logical_not(is_first))
    def _():
        x = x_ref[...].astype(BF16)
        acc = None
        for c in range(n_chunks):
            part = ffn_chunk(x, c)
            acc = part if acc is None else acc + part
        y_ref[...] = acc


def moe_ffn(xs, tile_expert, tile_first, tile_next, n_used, wg, wu, wd, layer, *, tm, tf=256):
    n_rows, d = xs.shape
    f_dim = wg.shape[3]
    tf = math.gcd(tf, f_dim)
    row_map = lambda i, te, first, nxt, nu: (jnp.minimum(i, nu[0] - 1), 0)
    any_spec = pl.BlockSpec(memory_space=pl.ANY)
    return pl.pallas_call(
        functools.partial(_moe_ffn_kernel, layer=layer, tf=tf),
        grid_spec=pltpu.PrefetchScalarGridSpec(
            num_scalar_prefetch=4,
            grid=(n_rows // tm,),
            in_specs=[pl.BlockSpec((tm, d), row_map), any_spec, any_spec, any_spec],
            out_specs=pl.BlockSpec((tm, d), lambda i, te, first, nxt, nu: (i, 0)),
            scratch_shapes=[
                pltpu.VMEM((d, f_dim), BF16), pltpu.VMEM((d, f_dim), BF16), pltpu.VMEM((f_dim, d), BF16),
                pltpu.VMEM((2, d, tf), F32), pltpu.VMEM((2, d, tf), F32), pltpu.VMEM((2, tf, d), F32),
                pltpu.SemaphoreType.DMA((3, 2)),
            ],
        ),
        out_shape=jax.ShapeDtypeStruct((n_rows, d), F32),
        compiler_params=_params(("arbitrary",), 56),
        name="moe_ffn",
    )(tile_expert, tile_first, tile_next, n_used, xs, wg, wu, wd)


def _combine_kernel(dest_ref, y_hbm, x_ref, gt_ref, g_ref, b_ref, o_ref, ob_ref, buf, sem,
                    *, tc, n_tiles, t):
    i = pl.program_id(0)

    def row_copy(src_row, slot, k, n):
        return pltpu.make_async_copy(y_hbm.at[pl.ds(src_row, 1)], buf.at[slot, k, pl.ds(n, 1)],
                                     sem.at[slot])

    def issue(tile, slot):
        def body(n, carry):
            for k in range(TOP_K):
                row_copy(dest_ref[k * t + tile * tc + n], slot, k, n).start(priority=k % 2)
            return carry
        lax.fori_loop(0, tc, body, 0, unroll=DMA_ISSUE_UNROLL)

    @pl.when(i == 0)
    def _():
        issue(0, 0)

    @pl.when(i + 1 < n_tiles)
    def _():
        issue(i + 1, (i + 1) % 2)

    slot = i % 2
    pltpu.make_async_copy(buf.at[slot], buf.at[slot], sem.at[slot]).wait()
    gt = gt_ref[...]
    mix = gt[:, 0:1] * buf[slot, 0] + gt[:, 1:2] * buf[slot, 1]
    out = _layer_norm(DEEPNORM_ALPHA * x_ref[...] + mix, g_ref[...], b_ref[...])
    o_ref[...] = out
    ob_ref[...] = out.astype(BF16)


def moe_combine(y, dest_flat, x, gates_t, ln_g, ln_b, *, tc=256):
    t, d = x.shape
    tc = min(tc, t)
    n_tiles = t // tc
    row = lambda v: v.reshape(1, d)
    blk = lambda i, dest: (i, 0)
    fixed = lambda i, dest: (0, 0)
    return pl.pallas_call(
        functools.partial(_combine_kernel, tc=tc, n_tiles=n_tiles, t=t),
        grid_spec=pltpu.PrefetchScalarGridSpec(
            num_scalar_prefetch=1,
            grid=(n_tiles,),
            in_specs=[
                pl.BlockSpec(memory_space=pl.ANY),
                pl.BlockSpec((tc, d), blk),
                pl.BlockSpec((tc, TOP_K), blk),
                pl.BlockSpec((1, d), fixed),
                pl.BlockSpec((1, d), fixed),
            ],
            out_specs=[pl.BlockSpec((tc, d), blk), pl.BlockSpec((tc, d), blk)],
            scratch_shapes=[pltpu.VMEM((2, TOP_K, tc, d), F32), pltpu.SemaphoreType.DMA((2,))],
        ),
        out_shape=[jax.ShapeDtypeStruct((t, d), F32), jax.ShapeDtypeStruct((t, d), BF16)],
        compiler_params=_params(("arbitrary",), 40),
        name="moe_combine",
    )(dest_flat, y, x, gates_t, row(ln_g), row(ln_b))


def moe_layer(x, router_w, router_b, wg, wu, wd, layer, ln_g, ln_b, *, tm=256):
    t, d = x.shape
    idx, gates, rank, cnt = moe_router(x, router_w, router_b)
    counts = cnt[:, 0]
    padded = (counts + tm - 1) // tm * tm
    pend = jnp.cumsum(padded).astype(jnp.int32)
    pstart = pend - padded
    experts = jnp.arange(N_EXPERTS, dtype=jnp.int32)
    seg_start = jnp.sum(jnp.where(idx[:, :, None] == experts, pstart, 0), axis=-1)
    dest = (seg_start + rank).reshape(TOP_K * t).astype(jnp.int32)
    n_tiles = (t * TOP_K) // tm + N_EXPERTS
    n_used = pend[-1] // tm
    tile_start = jnp.arange(n_tiles, dtype=jnp.int32) * tm
    te = jnp.sum((pend[None, :] <= tile_start[:, None]).astype(jnp.int32), axis=1)
    last_used = jnp.sum((pend <= (n_used - 1) * tm).astype(jnp.int32))
    te = jnp.where(tile_start < pend[-1], te, last_used).astype(jnp.int32)
    first = jnp.concatenate([jnp.ones((1,), jnp.int32), (te[1:] != te[:-1]).astype(jnp.int32)])
    later_nonempty = (counts[None, :] > 0) & (experts[None, :] > experts[:, None])
    next_expert = jnp.min(jnp.where(later_nonempty, experts[None, :], N_EXPERTS), axis=1)
    next_expert = jnp.where(next_expert < N_EXPERTS, next_expert, -1)
    tile_next = jnp.sum(jnp.where(te[:, None] == experts, next_expert, 0), axis=1).astype(jnp.int32)

    xs = moe_dispatch(x, dest, pend, n_tiles * tm, tm=tm)
    y = moe_ffn(xs, te, first, tile_next, n_used.reshape(1), wg, wu, wd, layer, tm=tm)
    return moe_combine(y, dest, x, gates.T, ln_g, ln_b)


def kernel(x, conv_w_pw1, conv_b_pw1, conv_w_dw, conv_b_dw, conv_ln_g, conv_ln_b, conv_w_pw2,
           conv_b_pw2, gdn_w_in, gdn_w_conv, gdn_a_log, gdn_dt_bias, gdn_norm_w, gdn_w_out,
           router_w, router_b, moe_w_gate, moe_w_up, moe_w_down, ln_mix_g, ln_mix_b, ln_ffn_g,
           ln_ffn_b):
    b_, s_, d = x.shape
    t = b_ * s_
    xf = x.reshape(t, d)
    xb = xf.astype(BF16)
    n_main = 6 * d
    n_vh = gdn_a_log.shape[1]
    kg = min(GDN_KEY_HEADS_PER_STEP, n_vh // 2)
    for i in range(DEPTH):
        j = i // 2
        if i % 2 == 0:
            h = pw1_glu(xb, conv_w_pw1, conv_b_pw1[j], j)
            hb = dwconv_ln_silu(h, conv_w_dw[j], conv_b_dw[j], conv_ln_g[j], conv_ln_b[j])
            xf, xb = mm_res_ln(hb, conv_w_pw2, j, conv_b_pw2[j], xf, ln_mix_g[i], ln_mix_b[i])
        else:
            proj = gdn_in_proj(xb, gdn_w_in, j, gdn_w_conv[j])
            col, gct = gdn_gates(xf, gdn_w_in[j, :, n_main:n_main + n_vh],
                                 gdn_w_in[j, :, n_main + n_vh:], gdn_a_log[j], gdn_dt_bias[j],
                                 hg=2 * kg)
            ob = gdn_chunk(proj, col, gct, gdn_norm_w[j], kg=kg)
            xf, xb = mm_res_ln(ob, gdn_w_out, j, jnp.zeros((d,), F32), xf, ln_mix_g[i], ln_mix_b[i])
        xf, xb = moe_layer(xf, router_w, router_b, moe_w_gate, moe_w_up, moe_w_down, i,
                           ln_ffn_g[i], ln_ffn_b[i])
    return xf.reshape(b_, s_, d)
```

```python
import functools
import math

import jax
import jax.numpy as jnp
from jax import lax
from jax.experimental import pallas as pl
from jax.experimental.pallas import tpu as pltpu

F32 = jnp.float32
BF16 = jnp.bfloat16

DEPTH = 4
DEEPNORM_ALPHA = (2 * DEPTH) ** 0.25
LN_EPS = 1e-5
NORM_EPS = 1e-6
HEAD_DIM = 128
GDN_CHUNK = 64
N_EXPERTS = 16
N_GROUPS = 4
EXPERTS_PER_GROUP = N_EXPERTS // N_GROUPS
TOP_K = 2
SUBLANES = 8
HALO_ROWS = 32
CARRY_ROWS = 8
MOE_WEIGHT_STAGES = 4
DMA_ISSUE_UNROLL = 8
GDN_KEY_HEADS_PER_STEP = 4
MIB = 1024 * 1024


def _params(semantics, vmem_mib):
    return pltpu.CompilerParams(dimension_semantics=semantics, vmem_limit_bytes=vmem_mib * MIB)


def _sigmoid(x):
    return 0.5 * jnp.tanh(0.5 * x) + 0.5


def _silu(x):
    return x * _sigmoid(x)


def _layer_norm(y, g, b):
    mu = jnp.mean(y, axis=-1, keepdims=True)
    d = y - mu
    var = jnp.mean(d * d, axis=-1, keepdims=True)
    return d * lax.rsqrt(var + LN_EPS) * g + b


def _split_bf16(x):
    hi = x.astype(BF16)
    lo = (x - hi.astype(F32)).astype(BF16)
    return hi, lo


def _dot(a, b):
    return jnp.dot(a, b, preferred_element_type=F32)


def _dot_nt(a, b):
    return lax.dot_general(a, b, (((1,), (1,)), ((), ())), preferred_element_type=F32)


def _dot_tn(a, b):
    return lax.dot_general(a, b, (((0,), (0,)), ((), ())), preferred_element_type=F32)


def _dot3(x, w):
    xh, xl = _split_bf16(x)
    wh, wl = _split_bf16(w)
    return _dot(xh, wh) + _dot(xl, wh) + _dot(xh, wl)


def _pw1_glu_kernel(x_ref, wv_ref, wg_ref, bv_ref, bg_ref, o_ref, wv_s, wg_s):
    @pl.when(pl.program_id(1) == 0)
    def _():
        wv_s[...] = wv_ref[...].astype(BF16)
        wg_s[...] = wg_ref[...].astype(BF16)

    x = x_ref[...]
    val = _dot(x, wv_s[...]) + bv_ref[...]
    gate = _dot(x, wg_s[...]) + bg_ref[...]
    o_ref[...] = val * _sigmoid(gate)


def pw1_glu(xb, w_all, b, layer, *, tm=1024, tn=512):
    m, k = xb.shape
    n = w_all.shape[2] // 2
    tm, tn = min(tm, m), min(tn, n)
    nj = n // tn
    b2 = b.reshape(1, 2 * n)
    w = w_all
    return pl.pallas_call(
        _pw1_glu_kernel,
        grid=(nj, m // tm),
        in_specs=[
            pl.BlockSpec((tm, k), lambda j, i: (i, 0)),
            pl.BlockSpec((None, k, tn), lambda j, i: (layer, 0, j)),
            pl.BlockSpec((None, k, tn), lambda j, i: (layer, 0, j + nj)),
            pl.BlockSpec((1, tn), lambda j, i: (0, j)),
            pl.BlockSpec((1, tn), lambda j, i: (0, j + nj)),
        ],
        out_specs=pl.BlockSpec((tm, tn), lambda j, i: (i, j)),
        out_shape=jax.ShapeDtypeStruct((m, n), F32),
        scratch_shapes=[pltpu.VMEM((k, tn), BF16), pltpu.VMEM((k, tn), BF16)],
        compiler_params=_params(("arbitrary", "arbitrary"), 48),
        name="pw1_glu",
    )(xb, w, w, b2, b2)


def _dwconv_ln_silu_kernel(h_ref, halo_ref, w_ref, b_ref, g_ref, beta_ref, o_ref, buf, cbuf,
                           *, ts, width, cw, rb):
    i = pl.program_id(0)
    d = h_ref.shape[1]
    buf[0:HALO_ROWS, :] = jnp.where(i > 0, halo_ref[...], 0.0)
    buf[HALO_ROWS:HALO_ROWS + ts, :] = h_ref[...]

    def col_body(c, carry):
        cols = pl.ds(pl.multiple_of(c * cw, cw), cw)
        for r in range(ts // rb):
            acc = jnp.broadcast_to(b_ref[:, cols], (rb, cw))
            for b in range(SUBLANES):
                yb = None
                for a in range((width - 1 - b) // SUBLANES + 1):
                    j = width - 1 - (SUBLANES * a + b)
                    start = r * rb + HALO_ROWS - SUBLANES * (a + 1)
                    term = w_ref[j:j + 1, cols] * buf[pl.ds(start, rb + SUBLANES), cols]
                    yb = term if yb is None else yb + term
                acc = acc + yb[SUBLANES - b:SUBLANES - b + rb]
            cbuf[pl.ds(r * rb, rb), cols] = acc
        return carry

    lax.fori_loop(0, d // cw, col_body, 0)
    y = _layer_norm(cbuf[...], g_ref[...], beta_ref[...])
    o_ref[...] = _silu(y).astype(o_ref.dtype)


def dwconv_ln_silu(h, w_dw, b_dw, ln_g, ln_b, *, ts=256, cw=256, rb=64):
    t, d = h.shape
    width = w_dw.shape[0]
    assert SUBLANES * ((width - 1) // SUBLANES + 1) <= HALO_ROWS
    ts, cw = min(ts, t), min(cw, d)
    halo_per_tile = ts // HALO_ROWS
    kern = functools.partial(_dwconv_ln_silu_kernel, ts=ts, width=width, cw=cw, rb=rb)
    row = lambda a: a.reshape(1, d)
    return pl.pallas_call(
        kern,
        grid=(t // ts,),
        in_specs=[
            pl.BlockSpec((ts, d), lambda i: (i, 0)),
            pl.BlockSpec((HALO_ROWS, d), lambda i: (jnp.maximum(i * halo_per_tile - 1, 0), 0)),
            pl.BlockSpec((width, d), lambda i: (0, 0)),
            pl.BlockSpec((1, d), lambda i: (0, 0)),
            pl.BlockSpec((1, d), lambda i: (0, 0)),
            pl.BlockSpec((1, d), lambda i: (0, 0)),
        ],
        out_specs=pl.BlockSpec((ts, d), lambda i: (i, 0)),
        out_shape=jax.ShapeDtypeStruct((t, d), BF16),
        scratch_shapes=[pltpu.VMEM((ts + HALO_ROWS, d), F32), pltpu.VMEM((ts, d), F32)],
        compiler_params=_params(("arbitrary",), 40),
        name="dwconv_ln_silu",
    )(h, h, w_dw, row(b_dw), row(ln_g), row(ln_b))


def _mm_res_ln_kernel(a_ref, w_hbm, bias_ref, res_ref, g_ref, b_ref, o_ref, ob_ref, wres, stage,
                      sem, *, layer, tk, sub):
    nk = wres.shape[0] // tk

    @pl.when(pl.program_id(0) == 0)
    def _():
        def chunk_copy(c):
            return pltpu.make_async_copy(w_hbm.at[layer, pl.ds(c * tk, tk)], stage.at[c % 2],
                                         sem.at[c % 2])
        chunk_copy(0).start()
        for c in range(nk):
            if c + 1 < nk:
                chunk_copy(c + 1).start()
            chunk_copy(c).wait()
            wres[c * tk:(c + 1) * tk, :] = stage[c % 2].astype(BF16)

    for sb in range(a_ref.shape[0] // sub):
        rows = slice(sb * sub, (sb + 1) * sub)
        y = DEEPNORM_ALPHA * res_ref[rows, :] + (_dot(a_ref[rows, :], wres[...]) + bias_ref[...])
        out = _layer_norm(y, g_ref[...], b_ref[...])
        o_ref[rows, :] = out
        ob_ref[rows, :] = out.astype(BF16)


def mm_res_ln(a, w_all, layer, bias, res, ln_g, ln_b, *, tm=256, tk=512, sub=128):
    m, k = a.shape
    n = w_all.shape[2]
    tm, tk = min(tm, m), min(tk, k)
    row = lambda v: v.reshape(1, n)
    return pl.pallas_call(
        functools.partial(_mm_res_ln_kernel, layer=layer, tk=tk, sub=min(sub, tm)),
        grid=(m // tm,),
        in_specs=[
            pl.BlockSpec((tm, k), lambda i: (i, 0)),
            pl.BlockSpec(memory_space=pl.ANY),
            pl.BlockSpec((1, n), lambda i: (0, 0)),
            pl.BlockSpec((tm, n), lambda i: (i, 0)),
            pl.BlockSpec((1, n), lambda i: (0, 0)),
            pl.BlockSpec((1, n), lambda i: (0, 0)),
        ],
        out_specs=[pl.BlockSpec((tm, n), lambda i: (i, 0)),
                   pl.BlockSpec((tm, n), lambda i: (i, 0))],
        out_shape=[jax.ShapeDtypeStruct((m, n), F32), jax.ShapeDtypeStruct((m, n), BF16)],
        scratch_shapes=[pltpu.VMEM((k, n), BF16), pltpu.VMEM((2, tk, n), F32),
                        pltpu.SemaphoreType.DMA((2,))],
        compiler_params=_params(("arbitrary",), 48),
        name="mm_res_ln",
    )(a, w_all, row(bias), res, row(ln_g), row(ln_b))


def _gdn_in_kernel(x_ref, w_ref, wc_ref, o_ref, wbf, pbuf, *, tm, tn, sub, nq, nqk, nconv, cwidth):
    j = pl.program_id(0)
    i = pl.program_id(1)
    heads = [slice(hh * HEAD_DIM, (hh + 1) * HEAD_DIM) for hh in range(tn // HEAD_DIM)]

    @pl.when(i == 0)
    def _():
        wbf[...] = w_ref[...].astype(BF16)
        pbuf[0:CARRY_ROWS, :] = jnp.zeros((CARRY_ROWS, tn), F32)

    def for_sub_blocks(epilogue):
        for sb in range(tm // sub):
            rows = slice(sb * sub, (sb + 1) * sub)
            epilogue(sb, rows, _dot_nt(x_ref[rows, :], wbf[...]))

    def conv_silu(sb, p):
        base = CARRY_ROWS + sb * sub
        pbuf[base:base + sub, :] = p
        c = None
        for jj in range(cwidth):
            term = wc_ref[jj:jj + 1, :] * pbuf[pl.ds(base - (cwidth - 1) + jj, sub), :]
            c = term if c is None else c + term
        return _silu(c)

    def store_heads(rows, s):
        for hh, cols in enumerate(heads):
            o_ref[hh, rows, :] = s[:, cols]

    def gate_epilogue(sb, rows, p):
        store_heads(rows, _silu(p))

    def value_epilogue(sb, rows, p):
        store_heads(rows, conv_silu(sb, p))

    def qk_epilogue(sb, rows, p):
        s = conv_silu(sb, p)
        scale = jnp.where(j < nq, HEAD_DIM ** -0.5, 1.0).astype(F32)
        for hh, cols in enumerate(heads):
            blk = s[:, cols]
            ss = jnp.sum(blk * blk, axis=-1, keepdims=True)
            o_ref[hh, rows, :] = blk * lax.rsqrt(ss + NORM_EPS) * scale

    @pl.when(j >= nconv)
    def _():
        for_sub_blocks(gate_epilogue)

    @pl.when((j >= nqk) & (j < nconv))
    def _():
        for_sub_blocks(value_epilogue)
        pbuf[0:CARRY_ROWS, :] = pbuf[tm:tm + CARRY_ROWS, :]

    @pl.when(j < nqk)
    def _():
        for_sub_blocks(qk_epilogue)
        pbuf[0:CARRY_ROWS, :] = pbuf[tm:tm + CARRY_ROWS, :]


def gdn_in_proj(xb, w_t_all, layer, w_conv, *, tm=1024, tn=512, sub=256):
    t, d = xb.shape
    qk_dim = d
    v_dim = 2 * d
    n_main = 2 * qk_dim + 2 * v_dim
    tm, tn = min(tm, t), min(tn, qk_dim)
    cwidth = w_conv.shape[0]
    nq = qk_dim // tn
    nconv = (2 * qk_dim + v_dim) // tn
    hpt = tn // HEAD_DIM
    kern = functools.partial(_gdn_in_kernel, tm=tm, tn=tn, sub=min(sub, tm), nq=nq, nqk=2 * nq,
                             nconv=nconv, cwidth=cwidth)
    return pl.pallas_call(
        kern,
        grid=(n_main // tn, t // tm),
        in_specs=[
            pl.BlockSpec((tm, d), lambda j, i: (i, 0)),
            pl.BlockSpec((None, tn, d), lambda j, i: (layer, j, 0)),
            pl.BlockSpec((cwidth, tn), lambda j, i: (0, jnp.minimum(j, nconv - 1))),
        ],
        out_specs=pl.BlockSpec((hpt, tm, HEAD_DIM), lambda j, i: (j, i, 0)),
        out_shape=jax.ShapeDtypeStruct((n_main // HEAD_DIM, t, HEAD_DIM), F32),
        scratch_shapes=[pltpu.VMEM((tn, d), BF16), pltpu.VMEM((tm + CARRY_ROWS, tn), F32)],
        compiler_params=_params(("arbitrary", "arbitrary"), 48),
        name="gdn_in_proj",
    )(xb, w_t_all, w_conv)


def _gdn_gate_kernel(x_ref, wt_ref, alog_ref, dtb_ref, col_ref, gct_ref, *, tm, hg):
    nh = alog_ref.shape[1]
    xh, xl = _split_bf16(x_ref[...])
    wh, wl = _split_bf16(wt_ref[...])
    proj = _dot_nt(xh, wh) + _dot_nt(xl, wh) + _dot_nt(xh, wl)
    beta = _sigmoid(proj[:, :nh])
    a = proj[:, nh:] + dtb_ref[...]
    softplus = jnp.maximum(a, 0.0) + jnp.log(1.0 + jnp.exp(-jnp.abs(a)))
    g = -jnp.exp(alog_ref[...]) * softplus
    r = lax.broadcasted_iota(jnp.int32, (tm, tm), 0)
    c = lax.broadcasted_iota(jnp.int32, (tm, tm), 1)
    same = r // GDN_CHUNK == c // GDN_CHUNK
    lower = jnp.where(same & (c <= r), 1.0, 0.0).astype(BF16)
    upper = jnp.where(same & (r <= c), 1.0, 0.0).astype(BF16)
    g1 = g.astype(BF16)
    r1 = g - g1.astype(F32)
    g2 = r1.astype(BF16)
    g3 = (r1 - g2.astype(F32)).astype(BF16)
    gc = _dot(lower, g1) + _dot(lower, g2) + _dot(lower, g3)
    gct_ref[...] = _dot_tn(g1, upper) + _dot_tn(g2, upper) + _dot_tn(g3, upper)
    for grp in range(col_ref.shape[0]):
        cols = slice(grp * hg, (grp + 1) * hg)
        col_ref[grp] = jnp.concatenate([beta[:, cols], gc[:, cols]], axis=1)


def gdn_gates(x, w_t_all, layer, row0, a_log, dt_bias, *, hg, tm=512):
    t, d = x.shape
    h = a_log.shape[0]
    tm = min(tm, t)
    row = lambda v: v.reshape(1, h)
    return pl.pallas_call(
        functools.partial(_gdn_gate_kernel, tm=tm, hg=hg),
        grid=(t // tm,),
        in_specs=[
            pl.BlockSpec((tm, d), lambda i: (i, 0)),
            pl.BlockSpec((None, 2 * h, d), lambda i: (layer, row0 // (2 * h), 0)),
            pl.BlockSpec((1, h), lambda i: (0, 0)),
            pl.BlockSpec((1, h), lambda i: (0, 0)),
        ],
        out_specs=[pl.BlockSpec((h // hg, tm, 2 * hg), lambda i: (0, i, 0)),
                   pl.BlockSpec((h, tm), lambda i: (0, i))],
        out_shape=[jax.ShapeDtypeStruct((h // hg, t, 2 * hg), F32),
                   jax.ShapeDtypeStruct((h, t), F32)],
        compiler_params=_params(("arbitrary",), 32),
        name="gdn_gates",
    )(x, w_t_all, row(a_log), row(dt_bias))


def _gdn_chunk_kernel(q_ref, k_ref, v_ref, z_ref, col_ref, row_ref, nw_ref, o_ref, state,
                      *, nc, kg):
    c_ = GDN_CHUNK
    hd = HEAD_DIM
    vh = 2 * kg

    @pl.when(pl.program_id(1) == 0)
    def _():
        state[...] = jnp.zeros(state.shape, F32)

    ii = lax.broadcasted_iota(jnp.int32, (c_, c_), 0)
    jj = lax.broadcasted_iota(jnp.int32, (c_, c_), 1)
    tril = ii >= jj
    strict = ii > jj

    q = q_ref[...].reshape(kg * nc, c_, hd)
    k = k_ref[...].reshape(kg * nc, c_, hd)
    kq = lax.dot_general(jnp.concatenate([k, q], axis=1).astype(BF16), k.astype(BF16),
                         (((2,), (2,)), ((0,), (0,))), preferred_element_type=F32)
    col = col_ref[...]
    neg_a, rhs, qk, k_dec, q_dec, g_last = [], [], [], [], [], []
    for h in range(vh):
        own = slice((h // 2) * nc, (h // 2 + 1) * nc)
        bcol = jnp.broadcast_to(col[:, h:h + 1], (nc * c_, hd)).reshape(nc, c_, hd)
        gcol = jnp.broadcast_to(col[:, vh + h:vh + h + 1], (nc * c_, hd)).reshape(nc, c_, hd)
        grow = row_ref[h]
        decay = jnp.where(tril, jnp.exp(jnp.minimum(gcol[:, :, :c_] - grow, 0.0)), 0.0)
        neg_a.append(jnp.where(strict, -(bcol[:, :, :c_] * kq[own, :c_] * decay), 0.0))
        qk.append((kq[own, c_:] * decay).astype(BF16))
        eg = jnp.exp(gcol)
        glast = gcol[:, c_ - 1:c_, :]
        rhs.append(jnp.concatenate([v_ref[h].reshape(nc, c_, hd) * bcol, k[own] * (bcol * eg)],
                                   axis=2))
        k_dec.append((k[own] * jnp.exp(glast - gcol)).astype(BF16))
        q_dec.append(q[own] * eg)
        g_last.append(jnp.exp(glast))

    def chunk_major(per_head):
        stacked = jnp.stack(per_head, axis=1)
        return stacked.reshape((nc * vh,) + stacked.shape[2:])

    pw = chunk_major(neg_a)
    rhs_all = chunk_major(rhs)
    half = (nc * vh) // 2
    n_steps = c_.bit_length() - 1
    bmm = lambda a, b: jnp.einsum("bij,bjk->bik", a, b, preferred_element_type=F32)
    row2 = lax.broadcasted_iota(jnp.int32, (c_, 2 * c_), 0)
    lane2 = lax.broadcasted_iota(jnp.int32, (c_, 2 * c_), 1)
    lane_lo = lane2 < c_
    eye2 = jnp.where((row2 == lane2) | (row2 + c_ == lane2), 1.0, 0.0)

    def block_diag(packed):
        return jnp.concatenate([jnp.where(lane_lo, packed, 0.0), jnp.where(lane_lo, 0.0, packed)],
                               axis=1).astype(BF16)

    q_pow = jnp.concatenate([pw[:half], pw[half:]], axis=2)
    t_inv = eye2 + q_pow
    for s in range(1, n_steps):
        q_pow = bmm(q_pow.astype(BF16), block_diag(q_pow))
        t_inv = t_inv + bmm(q_pow.astype(BF16), block_diag(t_inv))
    stacked_rhs = jnp.concatenate([rhs_all[:half], rhs_all[half:]], axis=1)
    corr = bmm(block_diag(t_inv - eye2), stacked_rhs.astype(BF16))
    sol = rhs_all + jnp.concatenate([corr[:, :c_], corr[:, c_:]], axis=0)
    u = sol[:, :, :hd]
    wq = jnp.concatenate([sol[:, :, hd:], chunk_major(q_dec)], axis=1).astype(BF16)
    qk = chunk_major(qk)
    k_dec = chunk_major(k_dec)
    g_last = chunk_major(g_last)

    nw = nw_ref[...]
    s_cur = [state[h] for h in range(vh)]
    for c in range(nc):
        rows = slice(c * c_, (c + 1) * c_)
        ws = [_dot(wq[c * vh + h], s_cur[h].astype(BF16)) for h in range(vh)]
        v_new = [(u[c * vh + h] - ws[h][:c_]).astype(BF16) for h in range(vh)]
        o = [ws[h][c_:] + _dot(qk[c * vh + h], v_new[h]) for h in range(vh)]
        s_cur = [s_cur[h] * g_last[c * vh + h] + _dot_tn(k_dec[c * vh + h], v_new[h])
                 for h in range(vh)]
        for h in range(vh):
            on = o[h] * lax.rsqrt(jnp.mean(o[h] * o[h], axis=-1, keepdims=True) + NORM_EPS) * nw
            o_ref[rows, h * hd:(h + 1) * hd] = (on * z_ref[h, rows, :]).astype(o_ref.dtype)
    for h in range(vh):
        state[h] = s_cur[h]


def gdn_chunk(proj, col, gct, norm_w, *, kg, tb=512):
    t = proj.shape[1]
    nkh = proj.shape[0] // 6
    vh = 2 * kg
    tb = min(tb, t)
    nc = tb // GDN_CHUNK
    n_chunks = t // GDN_CHUNK
    row = gct.reshape(nkh // kg, vh, n_chunks, 1, GDN_CHUNK)
    return pl.pallas_call(
        functools.partial(_gdn_chunk_kernel, nc=nc, kg=kg),
        grid=(nkh // kg, t // tb),
        in_specs=[
            pl.BlockSpec((kg, tb, HEAD_DIM), lambda g, n: (g, n, 0)),
            pl.BlockSpec((kg, tb, HEAD_DIM), lambda g, n: (nkh // kg + g, n, 0)),
            pl.BlockSpec((vh, tb, HEAD_DIM), lambda g, n: (2 * nkh // vh + g, n, 0)),
            pl.BlockSpec((vh, tb, HEAD_DIM), lambda g, n: (4 * nkh // vh + g, n, 0)),
            pl.BlockSpec((None, tb, 2 * vh), lambda g, n: (g, n, 0)),
            pl.BlockSpec((None, vh, nc, 1, GDN_CHUNK), lambda g, n: (g, 0, n, 0, 0)),
            pl.BlockSpec((1, HEAD_DIM), lambda g, n: (0, 0)),
        ],
        out_specs=pl.BlockSpec((tb, vh * HEAD_DIM), lambda g, n: (n, g)),
        out_shape=jax.ShapeDtypeStruct((t, 2 * nkh * HEAD_DIM), BF16),
        scratch_shapes=[pltpu.VMEM((vh, HEAD_DIM, HEAD_DIM), F32)],
        compiler_params=_params(("arbitrary", "arbitrary"), 48),
        name="gdn_chunk",
    )(proj, proj, proj, proj, col, row, norm_w.reshape(1, HEAD_DIM))


def _top2_of_4(a, b, c, d):
    hi1, lo1 = jnp.maximum(a, b), jnp.minimum(a, b)
    hi2, lo2 = jnp.maximum(c, d), jnp.minimum(c, d)
    return jnp.maximum(hi1, hi2), jnp.maximum(jnp.minimum(hi1, hi2), jnp.maximum(lo1, lo2))


def _first_argmax(vals):
    best, arg = vals[0], jnp.zeros(vals[0].shape, jnp.int32)
    for n in range(1, len(vals)):
        upd = vals[n] > best
        arg = jnp.where(upd, n, arg)
        best = jnp.where(upd, vals[n], best)
    return best, arg


def _router_kernel(x_ref, w_ref, b_ref, idx_ref, gate_ref, rank_ref, cnt_ref, run, *, tm):
    @pl.when(pl.program_id(0) == 0)
    def _():
        run[...] = jnp.zeros(run.shape, F32)

    xh, xl = _split_bf16(x_ref[...])
    wh, wl = _split_bf16(w_ref[...])
    logits = _dot_nt(wh, xh) + _dot_nt(wh, xl) + _dot_nt(wl, xh) + b_ref[...]
    e = jnp.exp(logits - jnp.max(logits, axis=0, keepdims=True))
    p = e / jnp.sum(e, axis=0, keepdims=True)
    rows = [p[r:r + 1, :] for r in range(N_EXPERTS)]
    scores = []
    for g in range(N_GROUPS):
        t1, t2 = _top2_of_4(*rows[g * EXPERTS_PER_GROUP:(g + 1) * EXPERTS_PER_GROUP])
        scores.append(t1 + t2)
    _, grp = _first_argmax(scores)
    vals = []
    for j in range(EXPERTS_PER_GROUP):
        v = rows[(N_GROUPS - 1) * EXPERTS_PER_GROUP + j]
        for g in range(N_GROUPS - 2, -1, -1):
            v = jnp.where(grp == g, rows[g * EXPERTS_PER_GROUP + j], v)
        vals.append(v)
    v1, i1 = _first_argmax(vals)
    v2, i2 = _first_argmax([jnp.where(i1 == j, -1.0, vals[j]) for j in range(EXPERTS_PER_GROUP)])
    e1 = grp * EXPERTS_PER_GROUP + i1
    e2 = grp * EXPERTS_PER_GROUP + i2
    den = v1 + v2
    idx_ref[0:1, :] = e1
    idx_ref[1:2, :] = e2
    gate_ref[0:1, :] = v1 / den
    gate_ref[1:2, :] = v2 / den

    eio = lax.broadcasted_iota(jnp.int32, (N_EXPERTS, tm), 0)
    hit1 = eio == e1
    hit2 = eio == e2
    onehot = jnp.where(hit1 | hit2, 1.0, 0.0)
    r = lax.broadcasted_iota(jnp.int32, (tm, tm), 0)
    c = lax.broadcasted_iota(jnp.int32, (tm, tm), 1)
    before = jnp.where(r < c, 1.0, 0.0).astype(BF16)
    prefix = _dot(onehot.astype(BF16), before) + run[:, 0:1]
    rank_ref[0:1, :] = jnp.sum(jnp.where(hit1, prefix, 0.0), axis=0, keepdims=True).astype(jnp.int32)
    rank_ref[1:2, :] = jnp.sum(jnp.where(hit2, prefix, 0.0), axis=0, keepdims=True).astype(jnp.int32)
    run[...] = run[...] + jnp.sum(onehot, axis=1, keepdims=True)
    cnt_ref[...] = run[...].astype(jnp.int32)


def moe_router(x, router_w, router_b, *, tm=512):
    t, d = x.shape
    tm = min(tm, t)
    out2 = lambda dt: jax.ShapeDtypeStruct((TOP_K, t), dt)
    blk2 = pl.BlockSpec((TOP_K, tm), lambda i: (0, i))
    return pl.pallas_call(
        functools.partial(_router_kernel, tm=tm),
        grid=(t // tm,),
        in_specs=[
            pl.BlockSpec((tm, d), lambda i: (i, 0)),
            pl.BlockSpec((N_EXPERTS, d), lambda i: (0, 0)),
            pl.BlockSpec((N_EXPERTS, 1), lambda i: (0, 0)),
        ],
        out_specs=[blk2, blk2, blk2, pl.BlockSpec((N_EXPERTS, 128), lambda i: (0, 0))],
        out_shape=[out2(jnp.int32), out2(F32), out2(jnp.int32),
                   jax.ShapeDtypeStruct((N_EXPERTS, 128), jnp.int32)],
        scratch_shapes=[pltpu.VMEM((N_EXPERTS, 128), F32)],
        compiler_params=_params(("arbitrary",), 32),
        name="moe_router",
    )(x, router_w.T, router_b.reshape(N_EXPERTS, 1))


def _dispatch_kernel(dest_ref, pend_ref, x_ref, xs_hbm, zbuf, sem, zsem, *, td, t, tm):
    base = pl.program_id(0) * td

    def zero_tile_copy(start):
        return pltpu.make_async_copy(zbuf, xs_hbm.at[pl.ds(pl.multiple_of(start, tm), tm)], zsem)

    @pl.when(pl.program_id(0) == 0)
    def _():
        zbuf[...] = jnp.zeros(zbuf.shape, zbuf.dtype)
        n_rows = xs_hbm.shape[0]
        starts, valid = [], []
        for e in range(N_EXPERTS):
            starts.append(pend_ref[e] - tm)
            valid.append(pend_ref[e] > (pend_ref[e - 1] if e else 0))
            starts.append(pend_ref[N_EXPERTS - 1] + e * tm)
            valid.append(starts[-1] < n_rows)
        for start, ok in zip(starts, valid):
            @pl.when(ok)
            def _():
                zero_tile_copy(start).start()
        for start, ok in zip(starts, valid):
            @pl.when(ok)
            def _():
                zero_tile_copy(start).wait()

    def row_copy(src_row, dst_row):
        return pltpu.make_async_copy(x_ref.at[pl.ds(src_row, 1)], xs_hbm.at[pl.ds(dst_row, 1)], sem)

    def issue(n, carry):
        for k in range(TOP_K):
            row_copy(n, dest_ref[k * t + base + n]).start()
        return carry

    lax.fori_loop(0, td, issue, 0, unroll=DMA_ISSUE_UNROLL)
    all_rows = xs_hbm.at[pl.ds(0, TOP_K * td)]
    pltpu.make_async_copy(all_rows, all_rows, sem).wait()


def moe_dispatch(x, dest_flat, pend, n_rows, *, tm, td=512):
    t, d = x.shape
    td = min(td, t)
    return pl.pallas_call(
        functools.partial(_dispatch_kernel, td=td, t=t, tm=tm),
        grid_spec=pltpu.PrefetchScalarGridSpec(
            num_scalar_prefetch=2,
            grid=(t // td,),
            in_specs=[pl.BlockSpec((td, d), lambda i, dest, pend: (i, 0))],
            out_specs=pl.BlockSpec(memory_space=pl.ANY),
            scratch_shapes=[pltpu.VMEM((tm, d), x.dtype), pltpu.SemaphoreType.DMA(()),
                            pltpu.SemaphoreType.DMA(())],
        ),
        out_shape=jax.ShapeDtypeStruct((n_rows, d), x.dtype),
        compiler_params=_params(("arbitrary",), 32),
        name="moe_dispatch",
    )(dest_flat, pend, x)


def _moe_ffn_kernel(te_ref, first_ref, next_ref, nu_ref, x_ref, wg_hbm, wu_hbm, wd_hbm, y_ref,
                    wg_res, wu_res, wd_res, wg_st, wu_st, wd_st, sem, *, layer, tf):
    i = pl.program_id(0)
    n_chunks = wg_res.shape[1] // tf
    n_stage = wg_st.shape[0]
    chunk = lambda c: slice(c * tf, (c + 1) * tf)

    def chunk_copies(expert, c):
        slot = c % n_stage
        return (
            pltpu.make_async_copy(wg_hbm.at[layer, expert, :, chunk(c)], wg_st.at[slot], sem.at[0, slot]),
            pltpu.make_async_copy(wu_hbm.at[layer, expert, :, chunk(c)], wu_st.at[slot], sem.at[1, slot]),
            pltpu.make_async_copy(wd_hbm.at[layer, expert, chunk(c), :], wd_st.at[slot], sem.at[2, slot]),
        )

    def start_chunk(expert, c):
        for copy in chunk_copies(expert, c):
            copy.start()

    def land_chunk(expert, c):
        for copy in chunk_copies(expert, c):
            copy.wait()
        wg_res[:, chunk(c)] = wg_st[c % n_stage].astype(BF16)
        wu_res[:, chunk(c)] = wu_st[c % n_stage].astype(BF16)
        wd_res[chunk(c), :] = wd_st[c % n_stage].astype(BF16)

    def ffn_chunk(x, c):
        h = _silu(_dot(x, wg_res[:, chunk(c)])) * _dot(x, wu_res[:, chunk(c)])
        return _dot(h.astype(BF16), wd_res[chunk(c), :])

    active = i < nu_ref[0]
    is_first = first_ref[i] == 1

    @pl.when(jnp.logical_not(active))
    def _():
        y_ref[...] = jnp.zeros(y_ref.shape, F32)

    @pl.when(i == 0)
    def _():
        for c in range(min(n_stage, n_chunks)):
            start_chunk(te_ref[0], c)

    @pl.when(active & is_first)
    def _():
        expert = te_ref[i]
        x = x_ref[...].astype(BF16)
        acc = None
        for c in range(n_chunks):
            land_chunk(expert, c)
            if c + n_stage < n_chunks:
                start_chunk(expert, c + n_stage)
            part = ffn_chunk(x, c)
            acc = part if acc is None else acc + part
        y_ref[...] = acc

        @pl.when(next_ref[i] >= 0)
        def _():
            for c in range(min(n_stage, n_chunks)):
                start_chunk(next_ref[i], c)

    @pl.when(active & jnp.logical_not(is_first))
    def _():
        x = x_ref[...].astype(BF16)
        acc = None
        for c in range(n_chunks):
            part = ffn_chunk(x, c)
            acc = part if acc is None else acc + part
        y_ref[...] = acc


def moe_ffn(xs, tile_expert, tile_first, tile_next, n_used, wg, wu, wd, layer, *, tm, tf=256):
    n_rows, d = xs.shape
    f_dim = wg.shape[3]
    tf = math.gcd(tf, f_dim)
    row_map = lambda i, te, first, nxt, nu: (jnp.minimum(i, nu[0] - 1), 0)
    any_spec = pl.BlockSpec(memory_space=pl.ANY)
    return pl.pallas_call(
        functools.partial(_moe_ffn_kernel, layer=layer, tf=tf),
        grid_spec=pltpu.PrefetchScalarGridSpec(
            num_scalar_prefetch=4,
            grid=(n_rows // tm,),
            in_specs=[pl.BlockSpec((tm, d), row_map), any_spec, any_spec, any_spec],
            out_specs=pl.BlockSpec((tm, d), lambda i, te, first, nxt, nu: (i, 0)),
            scratch_shapes=[
                pltpu.VMEM((d, f_dim), BF16), pltpu.VMEM((d, f_dim), BF16), pltpu.VMEM((f_dim, d), BF16),
                pltpu.VMEM((MOE_WEIGHT_STAGES, d, tf), F32), pltpu.VMEM((MOE_WEIGHT_STAGES, d, tf), F32),
                pltpu.VMEM((MOE_WEIGHT_STAGES, tf, d), F32),
                pltpu.SemaphoreType.DMA((3, MOE_WEIGHT_STAGES)),
            ],
        ),
        out_shape=jax.ShapeDtypeStruct((n_rows, d), F32),
        compiler_params=_params(("arbitrary",), 60),
        name="moe_ffn",
    )(tile_expert, tile_first, tile_next, n_used, xs, wg, wu, wd)


def _combine_kernel(dest_ref, y_hbm, x_ref, gt_ref, g_ref, b_ref, o_ref, ob_ref, buf, sem,
                    *, tc, n_tiles, t):
    i = pl.program_id(0)

    def row_copy(src_row, slot, k, n):
        return pltpu.make_async_copy(y_hbm.at[pl.ds(src_row, 1)], buf.at[slot, k, pl.ds(n, 1)],
                                     sem.at[slot])

    def issue(tile, slot):
        def body(n, carry):
            for k in range(TOP_K):
                row_copy(dest_ref[k * t + tile * tc + n], slot, k, n).start()
            return carry
        lax.fori_loop(0, tc, body, 0, unroll=DMA_ISSUE_UNROLL)

    @pl.when(i == 0)
    def _():
        issue(0, 0)

    @pl.when(i + 1 < n_tiles)
    def _():
        issue(i + 1, (i + 1) % 2)

    slot = i % 2
    pltpu.make_async_copy(buf.at[slot], buf.at[slot], sem.at[slot]).wait()
    gt = gt_ref[...]
    mix = gt[:, 0:1] * buf[slot, 0] + gt[:, 1:2] * buf[slot, 1]
    out = _layer_norm(DEEPNORM_ALPHA * x_ref[...] + mix, g_ref[...], b_ref[...])
    o_ref[...] = out
    ob_ref[...] = out.astype(BF16)


def moe_combine(y, dest_flat, x, gates_t, ln_g, ln_b, *, tc=256):
    t, d = x.shape
    tc = min(tc, t)
    n_tiles = t // tc
    row = lambda v: v.reshape(1, d)
    blk = lambda i, dest: (i, 0)
    fixed = lambda i, dest: (0, 0)
    return pl.pallas_call(
        functools.partial(_combine_kernel, tc=tc, n_tiles=n_tiles, t=t),
        grid_spec=pltpu.PrefetchScalarGridSpec(
            num_scalar_prefetch=1,
            grid=(n_tiles,),
            in_specs=[
                pl.BlockSpec(memory_space=pl.ANY),
                pl.BlockSpec((tc, d), blk),
                pl.BlockSpec((tc, TOP_K), blk),
                pl.BlockSpec((1, d), fixed),
                pl.BlockSpec((1, d), fixed),
            ],
            out_specs=[pl.BlockSpec((tc, d), blk), pl.BlockSpec((tc, d), blk)],
            scratch_shapes=[pltpu.VMEM((2, TOP_K, tc, d), F32), pltpu.SemaphoreType.DMA((2,))],
        ),
        out_shape=[jax.ShapeDtypeStruct((t, d), F32), jax.ShapeDtypeStruct((t, d), BF16)],
        compiler_params=_params(("arbitrary",), 40),
        name="moe_combine",
    )(dest_flat, y, x, gates_t, row(ln_g), row(ln_b))


def moe_layer(x, router_w, router_b, wg, wu, wd, layer, ln_g, ln_b, *, tm=256):
    t, d = x.shape
    idx, gates, rank, cnt = moe_router(x, router_w, router_b)
    counts = cnt[:, 0]
    padded = (counts + tm - 1) // tm * tm
    pend = jnp.cumsum(padded).astype(jnp.int32)
    pstart = pend - padded
    experts = jnp.arange(N_EXPERTS, dtype=jnp.int32)
    seg_start = jnp.sum(jnp.where(idx[:, :, None] == experts, pstart, 0), axis=-1)
    dest = (seg_start + rank).reshape(TOP_K * t).astype(jnp.int32)
    n_tiles = (t * TOP_K) // tm + N_EXPERTS
    n_used = pend[-1] // tm
    tile_start = jnp.arange(n_tiles, dtype=jnp.int32) * tm
    te = jnp.sum((pend[None, :] <= tile_start[:, None]).astype(jnp.int32), axis=1)
    last_used = jnp.sum((pend <= (n_used - 1) * tm).astype(jnp.int32))
    te = jnp.where(tile_start < pend[-1], te, last_used).astype(jnp.int32)
    first = jnp.concatenate([jnp.ones((1,), jnp.int32), (te[1:] != te[:-1]).astype(jnp.int32)])
    later_nonempty = (counts[None, :] > 0) & (experts[None, :] > experts[:, None])
    next_expert = jnp.min(jnp.where(later_nonempty, experts[None, :], N_EXPERTS), axis=1)
    next_expert = jnp.where(next_expert < N_EXPERTS, next_expert, -1)
    tile_next = jnp.sum(jnp.where(te[:, None] == experts, next_expert, 0), axis=1).astype(jnp.int32)

    xs = moe_dispatch(x, dest, pend, n_tiles * tm, tm=tm)
    y = moe_ffn(xs, te, first, tile_next, n_used.reshape(1), wg, wu, wd, layer, tm=tm)
    return moe_combine(y, dest, x, gates.T, ln_g, ln_b)


def kernel(x, conv_w_pw1, conv_b_pw1, conv_w_dw, conv_b_dw, conv_ln_g, conv_ln_b, conv_w_pw2,
           conv_b_pw2, gdn_w_in, gdn_w_conv, gdn_a_log, gdn_dt_bias, gdn_norm_w, gdn_w_out,
           router_w, router_b, moe_w_gate, moe_w_up, moe_w_down, ln_mix_g, ln_mix_b, ln_ffn_g,
           ln_ffn_b):
    b_, s_, d = x.shape
    t = b_ * s_
    xf = x.reshape(t, d)
    xb = xf.astype(BF16)
    n_main = 6 * d
    n_vh = gdn_a_log.shape[1]
    kg = min(GDN_KEY_HEADS_PER_STEP, n_vh // 2)
    w_in_t = jnp.swapaxes(gdn_w_in, 1, 2)
    for i in range(DEPTH):
        j = i // 2
        if i % 2 == 0:
            h = pw1_glu(xb, conv_w_pw1, conv_b_pw1[j], j)
            hb = dwconv_ln_silu(h, conv_w_dw[j], conv_b_dw[j], conv_ln_g[j], conv_ln_b[j])
            xf, xb = mm_res_ln(hb, conv_w_pw2, j, conv_b_pw2[j], xf, ln_mix_g[i], ln_mix_b[i])
        else:
            proj = gdn_in_proj(xb, w_in_t, j, gdn_w_conv[j])
            col, gct = gdn_gates(xf, w_in_t, j, n_main, gdn_a_log[j], gdn_dt_bias[j], hg=2 * kg)
            ob = gdn_chunk(proj, col, gct, gdn_norm_w[j], kg=kg)
            xf, xb = mm_res_ln(ob, gdn_w_out, j, jnp.zeros((d,), F32), xf, ln_mix_g[i], ln_mix_b[i])
        xf, xb = moe_layer(xf, router_w, router_b, moe_w_gate, moe_w_up, moe_w_down, i,
                           ln_ffn_g[i], ln_ffn_b[i])
    return xf.reshape(b_, s_, d)
```

```python
import functools
import math

import jax
import jax.numpy as jnp
from jax import lax
from jax.experimental import pallas as pl
from jax.experimental.pallas import tpu as pltpu

F32 = jnp.float32
BF16 = jnp.bfloat16

DEPTH = 4
DEEPNORM_ALPHA = (2 * DEPTH) ** 0.25
LN_EPS = 1e-5
NORM_EPS = 1e-6
HEAD_DIM = 128
GDN_CHUNK = 64
N_EXPERTS = 16
N_GROUPS = 4
EXPERTS_PER_GROUP = N_EXPERTS // N_GROUPS
TOP_K = 2
SUBLANES = 8
HALO_ROWS = 32
CARRY_ROWS = 8
MOE_WEIGHT_STAGES = 4
DMA_ISSUE_UNROLL = 8
GDN_KEY_HEADS_PER_STEP = 4
MIB = 1024 * 1024


def _params(semantics, vmem_mib):
    return pltpu.CompilerParams(dimension_semantics=semantics, vmem_limit_bytes=vmem_mib * MIB)


def _sigmoid(x):
    return 0.5 * jnp.tanh(0.5 * x) + 0.5


def _silu(x):
    return x * _sigmoid(x)


def _layer_norm(y, g, b):
    mu = jnp.mean(y, axis=-1, keepdims=True)
    d = y - mu
    var = jnp.mean(d * d, axis=-1, keepdims=True)
    return d * lax.rsqrt(var + LN_EPS) * g + b


def _split_bf16(x):
    hi = x.astype(BF16)
    lo = (x - hi.astype(F32)).astype(BF16)
    return hi, lo


def _dot(a, b):
    return jnp.dot(a, b, preferred_element_type=F32)


def _dot_nt(a, b):
    return lax.dot_general(a, b, (((1,), (1,)), ((), ())), preferred_element_type=F32)


def _dot_tn(a, b):
    return lax.dot_general(a, b, (((0,), (0,)), ((), ())), preferred_element_type=F32)


def _dot3(x, w):
    xh, xl = _split_bf16(x)
    wh, wl = _split_bf16(w)
    return _dot(xh, wh) + _dot(xl, wh) + _dot(xh, wl)


def _pw1_glu_kernel(x_ref, wv_ref, wg_ref, bv_ref, bg_ref, o_ref, wv_s, wg_s):
    @pl.when(pl.program_id(1) == 0)
    def _():
        wv_s[...] = wv_ref[...].astype(BF16)
        wg_s[...] = wg_ref[...].astype(BF16)

    x = x_ref[...]
    val = _dot(x, wv_s[...]) + bv_ref[...]
    gate = _dot(x, wg_s[...]) + bg_ref[...]
    o_ref[...] = val * _sigmoid(gate)


def pw1_glu(xb, w_all, b, layer, *, tm=1024, tn=512):
    m, k = xb.shape
    n = w_all.shape[2] // 2
    tm, tn = min(tm, m), min(tn, n)
    nj = n // tn
    b2 = b.reshape(1, 2 * n)
    w = w_all
    return pl.pallas_call(
        _pw1_glu_kernel,
        grid=(nj, m // tm),
        in_specs=[
            pl.BlockSpec((tm, k), lambda j, i: (i, 0)),
            pl.BlockSpec((None, k, tn), lambda j, i: (layer, 0, j)),
            pl.BlockSpec((None, k, tn), lambda j, i: (layer, 0, j + nj)),
            pl.BlockSpec((1, tn), lambda j, i: (0, j)),
            pl.BlockSpec((1, tn), lambda j, i: (0, j + nj)),
        ],
        out_specs=pl.BlockSpec((tm, tn), lambda j, i: (i, j)),
        out_shape=jax.ShapeDtypeStruct((m, n), F32),
        scratch_shapes=[pltpu.VMEM((k, tn), BF16), pltpu.VMEM((k, tn), BF16)],
        compiler_params=_params(("arbitrary", "arbitrary"), 48),
        name="pw1_glu",
    )(xb, w, w, b2, b2)


def _dwconv_ln_silu_kernel(h_ref, halo_ref, w_ref, b_ref, g_ref, beta_ref, o_ref, buf, cbuf,
                           *, ts, width, cw, rb):
    i = pl.program_id(0)
    d = h_ref.shape[1]
    buf[0:HALO_ROWS, :] = jnp.where(i > 0, halo_ref[...], 0.0)
    buf[HALO_ROWS:HALO_ROWS + ts, :] = h_ref[...]

    def col_body(c, carry):
        cols = pl.ds(pl.multiple_of(c * cw, cw), cw)
        for r in range(ts // rb):
            acc = jnp.broadcast_to(b_ref[:, cols], (rb, cw))
            for b in range(SUBLANES):
                yb = None
                for a in range((width - 1 - b) // SUBLANES + 1):
                    j = width - 1 - (SUBLANES * a + b)
                    start = r * rb + HALO_ROWS - SUBLANES * (a + 1)
                    term = w_ref[j:j + 1, cols] * buf[pl.ds(start, rb + SUBLANES), cols]
                    yb = term if yb is None else yb + term
                acc = acc + yb[SUBLANES - b:SUBLANES - b + rb]
            cbuf[pl.ds(r * rb, rb), cols] = acc
        return carry

    lax.fori_loop(0, d // cw, col_body, 0)
    y = _layer_norm(cbuf[...], g_ref[...], beta_ref[...])
    o_ref[...] = _silu(y).astype(o_ref.dtype)


def dwconv_ln_silu(h, w_dw, b_dw, ln_g, ln_b, *, ts=256, cw=128, rb=128):
    t, d = h.shape
    width = w_dw.shape[0]
    assert SUBLANES * ((width - 1) // SUBLANES + 1) <= HALO_ROWS
    ts, cw = min(ts, t), min(cw, d)
    halo_per_tile = ts // HALO_ROWS
    kern = functools.partial(_dwconv_ln_silu_kernel, ts=ts, width=width, cw=cw, rb=rb)
    row = lambda a: a.reshape(1, d)
    return pl.pallas_call(
        kern,
        grid=(t // ts,),
        in_specs=[
            pl.BlockSpec((ts, d), lambda i: (i, 0)),
            pl.BlockSpec((HALO_ROWS, d), lambda i: (jnp.maximum(i * halo_per_tile - 1, 0), 0)),
            pl.BlockSpec((width, d), lambda i: (0, 0)),
            pl.BlockSpec((1, d), lambda i: (0, 0)),
            pl.BlockSpec((1, d), lambda i: (0, 0)),
            pl.BlockSpec((1, d), lambda i: (0, 0)),
        ],
        out_specs=pl.BlockSpec((ts, d), lambda i: (i, 0)),
        out_shape=jax.ShapeDtypeStruct((t, d), BF16),
        scratch_shapes=[pltpu.VMEM((ts + HALO_ROWS, d), F32), pltpu.VMEM((ts, d), F32)],
        compiler_params=_params(("arbitrary",), 40),
        name="dwconv_ln_silu",
    )(h, h, w_dw, row(b_dw), row(ln_g), row(ln_b))


def _mm_res_ln_kernel(a_ref, w_hbm, bias_ref, res_ref, g_ref, b_ref, o_ref, ob_ref, wres, stage,
                      sem, *, layer, tk, sub):
    nk = wres.shape[0] // tk

    @pl.when(pl.program_id(0) == 0)
    def _():
        def chunk_copy(c):
            return pltpu.make_async_copy(w_hbm.at[layer, pl.ds(c * tk, tk)], stage.at[c % 2],
                                         sem.at[c % 2])
        chunk_copy(0).start()
        for c in range(nk):
            if c + 1 < nk:
                chunk_copy(c + 1).start()
            chunk_copy(c).wait()
            wres[c * tk:(c + 1) * tk, :] = stage[c % 2].astype(BF16)

    for sb in range(a_ref.shape[0] // sub):
        rows = slice(sb * sub, (sb + 1) * sub)
        y = DEEPNORM_ALPHA * res_ref[rows, :] + (_dot(a_ref[rows, :], wres[...]) + bias_ref[...])
        out = _layer_norm(y, g_ref[...], b_ref[...])
        o_ref[rows, :] = out
        ob_ref[rows, :] = out.astype(BF16)


def mm_res_ln(a, w_all, layer, bias, res, ln_g, ln_b, *, tm=256, tk=512, sub=128):
    m, k = a.shape
    n = w_all.shape[2]
    tm, tk = min(tm, m), min(tk, k)
    row = lambda v: v.reshape(1, n)
    return pl.pallas_call(
        functools.partial(_mm_res_ln_kernel, layer=layer, tk=tk, sub=min(sub, tm)),
        grid=(m // tm,),
        in_specs=[
            pl.BlockSpec((tm, k), lambda i: (i, 0)),
            pl.BlockSpec(memory_space=pl.ANY),
            pl.BlockSpec((1, n), lambda i: (0, 0)),
            pl.BlockSpec((tm, n), lambda i: (i, 0)),
            pl.BlockSpec((1, n), lambda i: (0, 0)),
            pl.BlockSpec((1, n), lambda i: (0, 0)),
        ],
        out_specs=[pl.BlockSpec((tm, n), lambda i: (i, 0)),
                   pl.BlockSpec((tm, n), lambda i: (i, 0))],
        out_shape=[jax.ShapeDtypeStruct((m, n), F32), jax.ShapeDtypeStruct((m, n), BF16)],
        scratch_shapes=[pltpu.VMEM((k, n), BF16), pltpu.VMEM((2, tk, n), F32),
                        pltpu.SemaphoreType.DMA((2,))],
        compiler_params=_params(("arbitrary",), 48),
        name="mm_res_ln",
    )(a, w_all, row(bias), res, row(ln_g), row(ln_b))


def _gdn_in_kernel(x_ref, w_ref, wc_ref, o_ref, wbf, pbuf, *, tm, tn, sub, nq, nqk, nconv, cwidth):
    j = pl.program_id(0)
    i = pl.program_id(1)
    heads = [slice(hh * HEAD_DIM, (hh + 1) * HEAD_DIM) for hh in range(tn // HEAD_DIM)]

    @pl.when(i == 0)
    def _():
        wbf[...] = w_ref[...].astype(BF16)
        pbuf[0:CARRY_ROWS, :] = jnp.zeros((CARRY_ROWS, tn), F32)

    def for_sub_blocks(epilogue):
        for sb in range(tm // sub):
            rows = slice(sb * sub, (sb + 1) * sub)
            epilogue(sb, rows, _dot_nt(x_ref[rows, :], wbf[...]))

    def conv_silu(sb, p):
        base = CARRY_ROWS + sb * sub
        pbuf[base:base + sub, :] = p
        c = None
        for jj in range(cwidth):
            term = wc_ref[jj:jj + 1, :] * pbuf[pl.ds(base - (cwidth - 1) + jj, sub), :]
            c = term if c is None else c + term
        return _silu(c)

    def store_heads(rows, s):
        for hh, cols in enumerate(heads):
            o_ref[hh, rows, :] = s[:, cols]

    def gate_epilogue(sb, rows, p):
        store_heads(rows, _silu(p))

    def value_epilogue(sb, rows, p):
        store_heads(rows, conv_silu(sb, p))

    def qk_epilogue(sb, rows, p):
        s = conv_silu(sb, p)
        scale = jnp.where(j < nq, HEAD_DIM ** -0.5, 1.0).astype(F32)
        for hh, cols in enumerate(heads):
            blk = s[:, cols]
            ss = jnp.sum(blk * blk, axis=-1, keepdims=True)
            o_ref[hh, rows, :] = blk * lax.rsqrt(ss + NORM_EPS) * scale

    @pl.when(j >= nconv)
    def _():
        for_sub_blocks(gate_epilogue)

    @pl.when((j >= nqk) & (j < nconv))
    def _():
        for_sub_blocks(value_epilogue)
        pbuf[0:CARRY_ROWS, :] = pbuf[tm:tm + CARRY_ROWS, :]

    @pl.when(j < nqk)
    def _():
        for_sub_blocks(qk_epilogue)
        pbuf[0:CARRY_ROWS, :] = pbuf[tm:tm + CARRY_ROWS, :]


def gdn_in_proj(xb, w_t_all, layer, w_conv, *, tm=1024, tn=512, sub=256):
    t, d = xb.shape
    qk_dim = d
    v_dim = 2 * d
    n_main = 2 * qk_dim + 2 * v_dim
    tm, tn = min(tm, t), min(tn, qk_dim)
    cwidth = w_conv.shape[0]
    nq = qk_dim // tn
    nconv = (2 * qk_dim + v_dim) // tn
    hpt = tn // HEAD_DIM
    kern = functools.partial(_gdn_in_kernel, tm=tm, tn=tn, sub=min(sub, tm), nq=nq, nqk=2 * nq,
                             nconv=nconv, cwidth=cwidth)
    return pl.pallas_call(
        kern,
        grid=(n_main // tn, t // tm),
        in_specs=[
            pl.BlockSpec((tm, d), lambda j, i: (i, 0)),
            pl.BlockSpec((None, tn, d), lambda j, i: (layer, j, 0)),
            pl.BlockSpec((cwidth, tn), lambda j, i: (0, jnp.minimum(j, nconv - 1))),
        ],
        out_specs=pl.BlockSpec((hpt, tm, HEAD_DIM), lambda j, i: (j, i, 0)),
        out_shape=jax.ShapeDtypeStruct((n_main // HEAD_DIM, t, HEAD_DIM), F32),
        scratch_shapes=[pltpu.VMEM((tn, d), BF16), pltpu.VMEM((tm + CARRY_ROWS, tn), F32)],
        compiler_params=_params(("arbitrary", "arbitrary"), 48),
        name="gdn_in_proj",
    )(xb, w_t_all, w_conv)


def _gdn_gate_kernel(x_ref, wt_ref, alog_ref, dtb_ref, col_ref, gct_ref, *, tm, hg):
    nh = alog_ref.shape[1]
    xh, xl = _split_bf16(x_ref[...])
    wh, wl = _split_bf16(wt_ref[...])
    proj = _dot_nt(xh, wh) + _dot_nt(xl, wh) + _dot_nt(xh, wl)
    beta = _sigmoid(proj[:, :nh])
    a = proj[:, nh:] + dtb_ref[...]
    softplus = jnp.maximum(a, 0.0) + jnp.log(1.0 + jnp.exp(-jnp.abs(a)))
    g = -jnp.exp(alog_ref[...]) * softplus
    r = lax.broadcasted_iota(jnp.int32, (tm, tm), 0)
    c = lax.broadcasted_iota(jnp.int32, (tm, tm), 1)
    same = r // GDN_CHUNK == c // GDN_CHUNK
    lower = jnp.where(same & (c <= r), 1.0, 0.0).astype(BF16)
    upper = jnp.where(same & (r <= c), 1.0, 0.0).astype(BF16)
    g1 = g.astype(BF16)
    r1 = g - g1.astype(F32)
    g2 = r1.astype(BF16)
    g3 = (r1 - g2.astype(F32)).astype(BF16)
    gc = _dot(lower, g1) + _dot(lower, g2) + _dot(lower, g3)
    gct_ref[...] = _dot_tn(g1, upper) + _dot_tn(g2, upper) + _dot_tn(g3, upper)
    for grp in range(col_ref.shape[0]):
        cols = slice(grp * hg, (grp + 1) * hg)
        col_ref[grp] = jnp.concatenate([beta[:, cols], gc[:, cols]], axis=1)


def gdn_gates(x, w_t_all, layer, row0, a_log, dt_bias, *, hg, tm=512):
    t, d = x.shape
    h = a_log.shape[0]
    tm = min(tm, t)
    row = lambda v: v.reshape(1, h)
    return pl.pallas_call(
        functools.partial(_gdn_gate_kernel, tm=tm, hg=hg),
        grid=(t // tm,),
        in_specs=[
            pl.BlockSpec((tm, d), lambda i: (i, 0)),
            pl.BlockSpec((None, 2 * h, d), lambda i: (layer, row0 // (2 * h), 0)),
            pl.BlockSpec((1, h), lambda i: (0, 0)),
            pl.BlockSpec((1, h), lambda i: (0, 0)),
        ],
        out_specs=[pl.BlockSpec((h // hg, tm, 2 * hg), lambda i: (0, i, 0)),
                   pl.BlockSpec((h, tm), lambda i: (0, i))],
        out_shape=[jax.ShapeDtypeStruct((h // hg, t, 2 * hg), F32),
                   jax.ShapeDtypeStruct((h, t), F32)],
        compiler_params=_params(("arbitrary",), 32),
        name="gdn_gates",
    )(x, w_t_all, row(a_log), row(dt_bias))


def _gdn_chunk_kernel(q_ref, k_ref, v_ref, z_ref, col_ref, row_ref, nw_ref, o_ref, state,
                      *, nc, kg):
    c_ = GDN_CHUNK
    hd = HEAD_DIM
    vh = 2 * kg

    @pl.when(pl.program_id(1) == 0)
    def _():
        state[...] = jnp.zeros(state.shape, F32)

    ii = lax.broadcasted_iota(jnp.int32, (c_, c_), 0)
    jj = lax.broadcasted_iota(jnp.int32, (c_, c_), 1)
    tril = ii >= jj
    strict = ii > jj

    q = q_ref[...].reshape(kg * nc, c_, hd)
    k = k_ref[...].reshape(kg * nc, c_, hd)
    kq = lax.dot_general(jnp.concatenate([k, q], axis=1).astype(BF16), k.astype(BF16),
                         (((2,), (2,)), ((0,), (0,))), preferred_element_type=F32)
    col = col_ref[...]
    neg_a, rhs, qk, k_dec, q_dec, g_last = [], [], [], [], [], []
    for h in range(vh):
        own = slice((h // 2) * nc, (h // 2 + 1) * nc)
        bcol = jnp.broadcast_to(col[:, h:h + 1], (nc * c_, hd)).reshape(nc, c_, hd)
        gcol = jnp.broadcast_to(col[:, vh + h:vh + h + 1], (nc * c_, hd)).reshape(nc, c_, hd)
        grow = row_ref[h]
        decay = jnp.where(tril, jnp.exp(jnp.minimum(gcol[:, :, :c_] - grow, 0.0)), 0.0)
        neg_a.append(jnp.where(strict, -(bcol[:, :, :c_] * kq[own, :c_] * decay), 0.0))
        qk.append((kq[own, c_:] * decay).astype(BF16))
        eg = jnp.exp(gcol)
        glast = gcol[:, c_ - 1:c_, :]
        rhs.append(jnp.concatenate([v_ref[h].reshape(nc, c_, hd) * bcol, k[own] * (bcol * eg)],
                                   axis=2))
        k_dec.append((k[own] * jnp.exp(glast - gcol)).astype(BF16))
        q_dec.append(q[own] * eg)
        g_last.append(jnp.exp(glast))

    def chunk_major(per_head):
        stacked = jnp.stack(per_head, axis=1)
        return stacked.reshape((nc * vh,) + stacked.shape[2:])

    pw = chunk_major(neg_a)
    rhs_all = chunk_major(rhs)
    half = (nc * vh) // 2
    n_steps = c_.bit_length() - 1
    bmm = lambda a, b: jnp.einsum("bij,bjk->bik", a, b, preferred_element_type=F32)
    row2 = lax.broadcasted_iota(jnp.int32, (c_, 2 * c_), 0)
    lane2 = lax.broadcasted_iota(jnp.int32, (c_, 2 * c_), 1)
    lane_lo = lane2 < c_
    eye2 = jnp.where((row2 == lane2) | (row2 + c_ == lane2), 1.0, 0.0)

    def block_diag(packed):
        return jnp.concatenate([jnp.where(lane_lo, packed, 0.0), jnp.where(lane_lo, 0.0, packed)],
                               axis=1).astype(BF16)

    q_pow = jnp.concatenate([pw[:half], pw[half:]], axis=2)
    t_inv = eye2 + q_pow
    for s in range(1, n_steps):
        q_pow = bmm(q_pow.astype(BF16), block_diag(q_pow))
        t_inv = t_inv + bmm(q_pow.astype(BF16), block_diag(t_inv))
    stacked_rhs = jnp.concatenate([rhs_all[:half], rhs_all[half:]], axis=1)
    corr = bmm(block_diag(t_inv - eye2), stacked_rhs.astype(BF16))
    sol = rhs_all + jnp.concatenate([corr[:, :c_], corr[:, c_:]], axis=0)
    u = sol[:, :, :hd]
    wq = jnp.concatenate([sol[:, :, hd:], chunk_major(q_dec)], axis=1).astype(BF16)
    qk = chunk_major(qk)
    k_dec = chunk_major(k_dec)
    g_last = chunk_major(g_last)

    nw = nw_ref[...]
    s_cur = [state[h] for h in range(vh)]
    for c in range(nc):
        rows = slice(c * c_, (c + 1) * c_)
        ws = [_dot(wq[c * vh + h], s_cur[h].astype(BF16)) for h in range(vh)]
        v_new = [(u[c * vh + h] - ws[h][:c_]).astype(BF16) for h in range(vh)]
        o = [ws[h][c_:] + _dot(qk[c * vh + h], v_new[h]) for h in range(vh)]
        s_cur = [s_cur[h] * g_last[c * vh + h] + _dot_tn(k_dec[c * vh + h], v_new[h])
                 for h in range(vh)]
        for h in range(vh):
            on = o[h] * lax.rsqrt(jnp.mean(o[h] * o[h], axis=-1, keepdims=True) + NORM_EPS) * nw
            o_ref[rows, h * hd:(h + 1) * hd] = (on * z_ref[h, rows, :]).astype(o_ref.dtype)
    for h in range(vh):
        state[h] = s_cur[h]


def gdn_chunk(proj, col, gct, norm_w, *, kg, tb=512):
    t = proj.shape[1]
    nkh = proj.shape[0] // 6
    vh = 2 * kg
    tb = min(tb, t)
    nc = tb // GDN_CHUNK
    n_chunks = t // GDN_CHUNK
    row = gct.reshape(nkh // kg, vh, n_chunks, 1, GDN_CHUNK)
    return pl.pallas_call(
        functools.partial(_gdn_chunk_kernel, nc=nc, kg=kg),
        grid=(nkh // kg, t // tb),
        in_specs=[
            pl.BlockSpec((kg, tb, HEAD_DIM), lambda g, n: (g, n, 0)),
            pl.BlockSpec((kg, tb, HEAD_DIM), lambda g, n: (nkh // kg + g, n, 0)),
            pl.BlockSpec((vh, tb, HEAD_DIM), lambda g, n: (2 * nkh // vh + g, n, 0)),
            pl.BlockSpec((vh, tb, HEAD_DIM), lambda g, n: (4 * nkh // vh + g, n, 0)),
            pl.BlockSpec((None, tb, 2 * vh), lambda g, n: (g, n, 0)),
            pl.BlockSpec((None, vh, nc, 1, GDN_CHUNK), lambda g, n: (g, 0, n, 0, 0)),
            pl.BlockSpec((1, HEAD_DIM), lambda g, n: (0, 0)),
        ],
        out_specs=pl.BlockSpec((tb, vh * HEAD_DIM), lambda g, n: (n, g)),
        out_shape=jax.ShapeDtypeStruct((t, 2 * nkh * HEAD_DIM), BF16),
        scratch_shapes=[pltpu.VMEM((vh, HEAD_DIM, HEAD_DIM), F32)],
        compiler_params=_params(("arbitrary", "arbitrary"), 48),
        name="gdn_chunk",
    )(proj, proj, proj, proj, col, row, norm_w.reshape(1, HEAD_DIM))


def _top2_of_4(a, b, c, d):
    hi1, lo1 = jnp.maximum(a, b), jnp.minimum(a, b)
    hi2, lo2 = jnp.maximum(c, d), jnp.minimum(c, d)
    return jnp.maximum(hi1, hi2), jnp.maximum(jnp.minimum(hi1, hi2), jnp.maximum(lo1, lo2))


def _first_argmax(vals):
    best, arg = vals[0], jnp.zeros(vals[0].shape, jnp.int32)
    for n in range(1, len(vals)):
        upd = vals[n] > best
        arg = jnp.where(upd, n, arg)
        best = jnp.where(upd, vals[n], best)
    return best, arg


def _router_kernel(x_ref, w_ref, b_ref, idx_ref, gate_ref, rank_ref, cnt_ref, run, *, tm):
    @pl.when(pl.program_id(0) == 0)
    def _():
        run[...] = jnp.zeros(run.shape, F32)

    xh, xl = _split_bf16(x_ref[...])
    wh, wl = _split_bf16(w_ref[...])
    logits = _dot_nt(wh, xh) + _dot_nt(wh, xl) + _dot_nt(wl, xh) + b_ref[...]
    e = jnp.exp(logits - jnp.max(logits, axis=0, keepdims=True))
    p = e / jnp.sum(e, axis=0, keepdims=True)
    rows = [p[r:r + 1, :] for r in range(N_EXPERTS)]
    scores = []
    for g in range(N_GROUPS):
        t1, t2 = _top2_of_4(*rows[g * EXPERTS_PER_GROUP:(g + 1) * EXPERTS_PER_GROUP])
        scores.append(t1 + t2)
    _, grp = _first_argmax(scores)
    vals = []
    for j in range(EXPERTS_PER_GROUP):
        v = rows[(N_GROUPS - 1) * EXPERTS_PER_GROUP + j]
        for g in range(N_GROUPS - 2, -1, -1):
            v = jnp.where(grp == g, rows[g * EXPERTS_PER_GROUP + j], v)
        vals.append(v)
    v1, i1 = _first_argmax(vals)
    v2, i2 = _first_argmax([jnp.where(i1 == j, -1.0, vals[j]) for j in range(EXPERTS_PER_GROUP)])
    e1 = grp * EXPERTS_PER_GROUP + i1
    e2 = grp * EXPERTS_PER_GROUP + i2
    den = v1 + v2
    idx_ref[0:1, :] = e1
    idx_ref[1:2, :] = e2
    gate_ref[0:1, :] = v1 / den
    gate_ref[1:2, :] = v2 / den

    eio = lax.broadcasted_iota(jnp.int32, (N_EXPERTS, tm), 0)
    hit1 = eio == e1
    hit2 = eio == e2
    onehot = jnp.where(hit1 | hit2, 1.0, 0.0)
    r = lax.broadcasted_iota(jnp.int32, (tm, tm), 0)
    c = lax.broadcasted_iota(jnp.int32, (tm, tm), 1)
    before = jnp.where(r < c, 1.0, 0.0).astype(BF16)
    prefix = _dot(onehot.astype(BF16), before) + run[:, 0:1]
    rank_ref[0:1, :] = jnp.sum(jnp.where(hit1, prefix, 0.0), axis=0, keepdims=True).astype(jnp.int32)
    rank_ref[1:2, :] = jnp.sum(jnp.where(hit2, prefix, 0.0), axis=0, keepdims=True).astype(jnp.int32)
    run[...] = run[...] + jnp.sum(onehot, axis=1, keepdims=True)
    cnt_ref[...] = run[...].astype(jnp.int32)


def moe_router(x, router_w, router_b, *, tm=512):
    t, d = x.shape
    tm = min(tm, t)
    out2 = lambda dt: jax.ShapeDtypeStruct((TOP_K, t), dt)
    blk2 = pl.BlockSpec((TOP_K, tm), lambda i: (0, i))
    return pl.pallas_call(
        functools.partial(_router_kernel, tm=tm),
        grid=(t // tm,),
        in_specs=[
            pl.BlockSpec((tm, d), lambda i: (i, 0)),
            pl.BlockSpec((N_EXPERTS, d), lambda i: (0, 0)),
            pl.BlockSpec((N_EXPERTS, 1), lambda i: (0, 0)),
        ],
        out_specs=[blk2, blk2, blk2, pl.BlockSpec((N_EXPERTS, 128), lambda i: (0, 0))],
        out_shape=[out2(jnp.int32), out2(F32), out2(jnp.int32),
                   jax.ShapeDtypeStruct((N_EXPERTS, 128), jnp.int32)],
        scratch_shapes=[pltpu.VMEM((N_EXPERTS, 128), F32)],
        compiler_params=_params(("arbitrary",), 32),
        name="moe_router",
    )(x, router_w.T, router_b.reshape(N_EXPERTS, 1))


def _dispatch_kernel(dest_ref, pend_ref, x_ref, xs_hbm, zbuf, sem, zsem, *, td, t, tm):
    base = pl.program_id(0) * td

    def zero_tile_copy(start):
        return pltpu.make_async_copy(zbuf, xs_hbm.at[pl.ds(pl.multiple_of(start, tm), tm)], zsem)

    @pl.when(pl.program_id(0) == 0)
    def _():
        zbuf[...] = jnp.zeros(zbuf.shape, zbuf.dtype)
        n_rows = xs_hbm.shape[0]
        starts, valid = [], []
        for e in range(N_EXPERTS):
            starts.append(pend_ref[e] - tm)
            valid.append(pend_ref[e] > (pend_ref[e - 1] if e else 0))
            starts.append(pend_ref[N_EXPERTS - 1] + e * tm)
            valid.append(starts[-1] < n_rows)
        for start, ok in zip(starts, valid):
            @pl.when(ok)
            def _():
                zero_tile_copy(start).start()
        for start, ok in zip(starts, valid):
            @pl.when(ok)
            def _():
                zero_tile_copy(start).wait()

    def row_copy(src_row, dst_row):
        return pltpu.make_async_copy(x_ref.at[pl.ds(src_row, 1)], xs_hbm.at[pl.ds(dst_row, 1)], sem)

    def issue(n, carry):
        for k in range(TOP_K):
            row_copy(n, dest_ref[k * t + base + n]).start()
        return carry

    lax.fori_loop(0, td, issue, 0, unroll=DMA_ISSUE_UNROLL)
    all_rows = xs_hbm.at[pl.ds(0, TOP_K * td)]
    pltpu.make_async_copy(all_rows, all_rows, sem).wait()


def moe_dispatch(x, dest_flat, pend, n_rows, *, tm, td=512):
    t, d = x.shape
    td = min(td, t)
    return pl.pallas_call(
        functools.partial(_dispatch_kernel, td=td, t=t, tm=tm),
        grid_spec=pltpu.PrefetchScalarGridSpec(
            num_scalar_prefetch=2,
            grid=(t // td,),
            in_specs=[pl.BlockSpec((td, d), lambda i, dest, pend: (i, 0))],
            out_specs=pl.BlockSpec(memory_space=pl.ANY),
            scratch_shapes=[pltpu.VMEM((tm, d), x.dtype), pltpu.SemaphoreType.DMA(()),
                            pltpu.SemaphoreType.DMA(())],
        ),
        out_shape=jax.ShapeDtypeStruct((n_rows, d), x.dtype),
        compiler_params=_params(("arbitrary",), 32),
        name="moe_dispatch",
    )(dest_flat, pend, x)


def _moe_ffn_kernel(te_ref, first_ref, next_ref, nu_ref, x_ref, wg_hbm, wu_hbm, wd_hbm, y_ref,
                    wg_res, wu_res, wd_res, wg_st, wu_st, wd_st, sem, *, layer, tf):
    i = pl.program_id(0)
    n_chunks = wg_res.shape[1] // tf
    n_stage = wg_st.shape[0]
    chunk = lambda c: slice(c * tf, (c + 1) * tf)

    def chunk_copies(expert, c):
        slot = c % n_stage
        return (
            pltpu.make_async_copy(wg_hbm.at[layer, expert, :, chunk(c)], wg_st.at[slot], sem.at[0, slot]),
            pltpu.make_async_copy(wu_hbm.at[layer, expert, :, chunk(c)], wu_st.at[slot], sem.at[1, slot]),
            pltpu.make_async_copy(wd_hbm.at[layer, expert, chunk(c), :], wd_st.at[slot], sem.at[2, slot]),
        )

    def start_chunk(expert, c):
        for copy in chunk_copies(expert, c):
            copy.start()

    def land_chunk(expert, c):
        for copy in chunk_copies(expert, c):
            copy.wait()
        wg_res[:, chunk(c)] = wg_st[c % n_stage].astype(BF16)
        wu_res[:, chunk(c)] = wu_st[c % n_stage].astype(BF16)
        wd_res[chunk(c), :] = wd_st[c % n_stage].astype(BF16)

    def ffn_chunk(x, c):
        h = _silu(_dot(x, wg_res[:, chunk(c)])) * _dot(x, wu_res[:, chunk(c)])
        return _dot(h.astype(BF16), wd_res[chunk(c), :])

    active = i < nu_ref[0]
    is_first = first_ref[i] == 1

    @pl.when(jnp.logical_not(active))
    def _():
        y_ref[...] = jnp.zeros(y_ref.shape, F32)

    @pl.when(i == 0)
    def _():
        for c in range(min(n_stage, n_chunks)):
            start_chunk(te_ref[0], c)

    @pl.when(active & is_first)
    def _():
        expert = te_ref[i]
        x = x_ref[...].astype(BF16)
        acc = None
        for c in range(n_chunks):
            land_chunk(expert, c)
            if c + n_stage < n_chunks:
                start_chunk(expert, c + n_stage)
            part = ffn_chunk(x, c)
            acc = part if acc is None else acc + part
        y_ref[...] = acc

        @pl.when(next_ref[i] >= 0)
        def _():
            for c in range(min(n_stage, n_chunks)):
                start_chunk(next_ref[i], c)

    @pl.when(active & jnp.logical_not(is_first))
    def _():
        x = x_ref[...].astype(BF16)
        acc = None
        for c in range(n_chunks):
            part = ffn_chunk(x, c)
            acc = part if acc is None else acc + part
        y_ref[...] = acc


def moe_ffn(xs, tile_expert, tile_first, tile_next, n_used, wg, wu, wd, layer, *, tm, tf=256):
    n_rows, d = xs.shape
    f_dim = wg.shape[3]
    tf = math.gcd(tf, f_dim)
    row_map = lambda i, te, first, nxt, nu: (jnp.minimum(i, nu[0] - 1), 0)
    any_spec = pl.BlockSpec(memory_space=pl.ANY)
    return pl.pallas_call(
        functools.partial(_moe_ffn_kernel, layer=layer, tf=tf),
        grid_spec=pltpu.PrefetchScalarGridSpec(
            num_scalar_prefetch=4,
            grid=(n_rows // tm,),
            in_specs=[pl.BlockSpec((tm, d), row_map), any_spec, any_spec, any_spec],
            out_specs=pl.BlockSpec((tm, d), lambda i, te, first, nxt, nu: (i, 0)),
            scratch_shapes=[
                pltpu.VMEM((d, f_dim), BF16), pltpu.VMEM((d, f_dim), BF16), pltpu.VMEM((f_dim, d), BF16),
                pltpu.VMEM((MOE_WEIGHT_STAGES, d, tf), F32), pltpu.VMEM((MOE_WEIGHT_STAGES, d, tf), F32),
                pltpu.VMEM((MOE_WEIGHT_STAGES, tf, d), F32),
                pltpu.SemaphoreType.DMA((3, MOE_WEIGHT_STAGES)),
            ],
        ),
        out_shape=jax.ShapeDtypeStruct((n_rows, d), F32),
        compiler_params=_params(("arbitrary",), 60),
        name="moe_ffn",
    )(tile_expert, tile_first, tile_next, n_used, xs, wg, wu, wd)


def _combine_kernel(dest_ref, y_hbm, x_ref, gt_ref, g_ref, b_ref, o_ref, ob_ref, buf, sem,
                    *, tc, n_tiles, t):
    i = pl.program_id(0)

    def row_copy(src_row, slot, k, n):
        return pltpu.make_async_copy(y_hbm.at[pl.ds(src_row, 1)], buf.at[slot, k, pl.ds(n, 1)],
                                     sem.at[slot])

    @pl.when(i == 0)
    def _():
        def body(n, carry):
            for k in range(TOP_K):
                row_copy(dest_ref[k * t + n], 0, k, n).start()
            return carry
        lax.fori_loop(0, tc, body, 0, unroll=DMA_ISSUE_UNROLL)

    def wait_rows(slot):
        pltpu.make_async_copy(buf.at[slot], buf.at[slot], sem.at[slot]).wait()

    def finish(slot):
        gt = gt_ref[...]
        mix = gt[:, 0:1] * buf[slot, 0] + gt[:, 1:2] * buf[slot, 1]
        out = _layer_norm(DEEPNORM_ALPHA * x_ref[...] + mix, g_ref[...], b_ref[...])
        o_ref[...] = out
        ob_ref[...] = out.astype(BF16)

    def gather_next_and_finish(slot):
        wait_rows(slot)
        base = (i + 1) * tc
        for n in range(tc):
            for k in range(TOP_K):
                row_copy(dest_ref[k * t + base + n], 1 - slot, k, n).start()
        finish(slot)

    has_next = i + 1 < n_tiles

    @pl.when(has_next & (i % 2 == 0))
    def _():
        gather_next_and_finish(0)

    @pl.when(has_next & (i % 2 == 1))
    def _():
        gather_next_and_finish(1)

    @pl.when(jnp.logical_not(has_next))
    def _():
        wait_rows((n_tiles - 1) % 2)
        finish((n_tiles - 1) % 2)


def moe_combine(y, dest_flat, x, gates_t, ln_g, ln_b, *, tc=256):
    t, d = x.shape
    tc = min(tc, t)
    n_tiles = t // tc
    row = lambda v: v.reshape(1, d)
    blk = lambda i, dest: (i, 0)
    fixed = lambda i, dest: (0, 0)
    return pl.pallas_call(
        functools.partial(_combine_kernel, tc=tc, n_tiles=n_tiles, t=t),
        grid_spec=pltpu.PrefetchScalarGridSpec(
            num_scalar_prefetch=1,
            grid=(n_tiles,),
            in_specs=[
                pl.BlockSpec(memory_space=pl.ANY),
                pl.BlockSpec((tc, d), blk),
                pl.BlockSpec((tc, TOP_K), blk),
                pl.BlockSpec((1, d), fixed),
                pl.BlockSpec((1, d), fixed),
            ],
            out_specs=[pl.BlockSpec((tc, d), blk), pl.BlockSpec((tc, d), blk)],
            scratch_shapes=[pltpu.VMEM((2, TOP_K, tc, d), F32), pltpu.SemaphoreType.DMA((2,))],
        ),
        out_shape=[jax.ShapeDtypeStruct((t, d), F32), jax.ShapeDtypeStruct((t, d), BF16)],
        compiler_params=_params(("arbitrary",), 40),
        name="moe_combine",
    )(dest_flat, y, x, gates_t, row(ln_g), row(ln_b))


def moe_layer(x, router_w, router_b, wg, wu, wd, layer, ln_g, ln_b, *, tm=256):
    t, d = x.shape
    idx, gates, rank, cnt = moe_router(x, router_w, router_b)
    counts = cnt[:, 0]
    padded = (counts + tm - 1) // tm * tm
    pend = jnp.cumsum(padded).astype(jnp.int32)
    pstart = pend - padded
    experts = jnp.arange(N_EXPERTS, dtype=jnp.int32)
    seg_start = jnp.sum(jnp.where(idx[:, :, None] == experts, pstart, 0), axis=-1)
    dest = (seg_start + rank).reshape(TOP_K * t).astype(jnp.int32)
    n_tiles = (t * TOP_K) // tm + N_EXPERTS
    n_used = pend[-1] // tm
    tile_start = jnp.arange(n_tiles, dtype=jnp.int32) * tm
    te = jnp.sum((pend[None, :] <= tile_start[:, None]).astype(jnp.int32), axis=1)
    last_used = jnp.sum((pend <= (n_used - 1) * tm).astype(jnp.int32))
    te = jnp.where(tile_start < pend[-1], te, last_used).astype(jnp.int32)
    first = jnp.concatenate([jnp.ones((1,), jnp.int32), (te[1:] != te[:-1]).astype(jnp.int32)])
    later_nonempty = (counts[None, :] > 0) & (experts[None, :] > experts[:, None])
    next_expert = jnp.min(jnp.where(later_nonempty, experts[None, :], N_EXPERTS), axis=1)
    next_expert = jnp.where(next_expert < N_EXPERTS, next_expert, -1)
    tile_next = jnp.sum(jnp.where(te[:, None] == experts, next_expert, 0), axis=1).astype(jnp.int32)

    xs = moe_dispatch(x, dest, pend, n_tiles * tm, tm=tm)
    y = moe_ffn(xs, te, first, tile_next, n_used.reshape(1), wg, wu, wd, layer, tm=tm)
    return moe_combine(y, dest, x, gates.T, ln_g, ln_b)


def kernel(x, conv_w_pw1, conv_b_pw1, conv_w_dw, conv_b_dw, conv_ln_g, conv_ln_b, conv_w_pw2,
           conv_b_pw2, gdn_w_in, gdn_w_conv, gdn_a_log, gdn_dt_bias, gdn_norm_w, gdn_w_out,
           router_w, router_b, moe_w_gate, moe_w_up, moe_w_down, ln_mix_g, ln_mix_b, ln_ffn_g,
           ln_ffn_b):
    b_, s_, d = x.shape
    t = b_ * s_
    xf = x.reshape(t, d)
    xb = xf.astype(BF16)
    n_main = 6 * d
    n_vh = gdn_a_log.shape[1]
    kg = min(GDN_KEY_HEADS_PER_STEP, n_vh // 2)
    w_in_t = jnp.swapaxes(gdn_w_in, 1, 2)
    for i in range(DEPTH):
        j = i // 2
        if i % 2 == 0:
            h = pw1_glu(xb, conv_w_pw1, conv_b_pw1[j], j)
            hb = dwconv_ln_silu(h, conv_w_dw[j], conv_b_dw[j], conv_ln_g[j], conv_ln_b[j])
            xf, xb = mm_res_ln(hb, conv_w_pw2, j, conv_b_pw2[j], xf, ln_mix_g[i], ln_mix_b[i])
        else:
            proj = gdn_in_proj(xb, w_in_t, j, gdn_w_conv[j])
            col, gct = gdn_gates(xf, w_in_t, j, n_main, gdn_a_log[j], gdn_dt_bias[j], hg=2 * kg)
            ob = gdn_chunk(proj, col, gct, gdn_norm_w[j], kg=kg)
            xf, xb = mm_res_ln(ob, gdn_w_out, j, jnp.zeros((d,), F32), xf, ln_mix_g[i], ln_mix_b[i])
        xf, xb = moe_layer(xf, router_w, router_b, moe_w_gate, moe_w_up, moe_w_down, i,
                           ln_ffn_g[i], ln_ffn_b[i])
    return xf.reshape(b_, s_, d)
```

```python
import functools
import math

import jax
import jax.numpy as jnp
from jax import lax
from jax.experimental import pallas as pl
from jax.experimental.pallas import tpu as pltpu

F32 = jnp.float32
BF16 = jnp.bfloat16

DEPTH = 4
DEEPNORM_ALPHA = (2 * DEPTH) ** 0.25
LN_EPS = 1e-5
NORM_EPS = 1e-6
HEAD_DIM = 128
GDN_CHUNK = 64
N_EXPERTS = 16
N_GROUPS = 4
EXPERTS_PER_GROUP = N_EXPERTS // N_GROUPS
TOP_K = 2
SUBLANES = 8
HALO_ROWS = 32
CARRY_ROWS = 8
MOE_WEIGHT_STAGES = 4
DMA_ISSUE_UNROLL = 8
GDN_SOLVE_GROUPS = 2
GDN_KEY_HEADS_PER_STEP = 4
MIB = 1024 * 1024


def _params(semantics, vmem_mib):
    return pltpu.CompilerParams(dimension_semantics=semantics, vmem_limit_bytes=vmem_mib * MIB)


def _sigmoid(x):
    return 0.5 * jnp.tanh(0.5 * x) + 0.5


def _silu(x):
    return x * _sigmoid(x)


def _layer_norm(y, g, b):
    mu = jnp.mean(y, axis=-1, keepdims=True)
    d = y - mu
    var = jnp.mean(d * d, axis=-1, keepdims=True)
    return d * lax.rsqrt(var + LN_EPS) * g + b


def _split_bf16(x):
    hi = x.astype(BF16)
    lo = (x - hi.astype(F32)).astype(BF16)
    return hi, lo


def _dot(a, b):
    return jnp.dot(a, b, preferred_element_type=F32)


def _dot_nt(a, b):
    return lax.dot_general(a, b, (((1,), (1,)), ((), ())), preferred_element_type=F32)


def _dot_tn(a, b):
    return lax.dot_general(a, b, (((0,), (0,)), ((), ())), preferred_element_type=F32)


def _dot3(x, w):
    xh, xl = _split_bf16(x)
    wh, wl = _split_bf16(w)
    return _dot(xh, wh) + _dot(xl, wh) + _dot(xh, wl)


def _pw1_glu_kernel(x_ref, wv_ref, wg_ref, bv_ref, bg_ref, o_ref, wv_s, wg_s):
    @pl.when(pl.program_id(1) == 0)
    def _():
        wv_s[...] = wv_ref[...].astype(BF16)
        wg_s[...] = wg_ref[...].astype(BF16)

    x = x_ref[...]
    val = _dot(x, wv_s[...]) + bv_ref[...]
    gate = _dot(x, wg_s[...]) + bg_ref[...]
    o_ref[...] = val * _sigmoid(gate)


def pw1_glu(xb, w_all, b, layer, *, tm=1024, tn=512):
    m, k = xb.shape
    n = w_all.shape[2] // 2
    tm, tn = min(tm, m), min(tn, n)
    nj = n // tn
    b2 = b.reshape(1, 2 * n)
    w = w_all
    return pl.pallas_call(
        _pw1_glu_kernel,
        grid=(nj, m // tm),
        in_specs=[
            pl.BlockSpec((tm, k), lambda j, i: (i, 0)),
            pl.BlockSpec((None, k, tn), lambda j, i: (layer, 0, j)),
            pl.BlockSpec((None, k, tn), lambda j, i: (layer, 0, j + nj)),
            pl.BlockSpec((1, tn), lambda j, i: (0, j)),
            pl.BlockSpec((1, tn), lambda j, i: (0, j + nj)),
        ],
        out_specs=pl.BlockSpec((tm, tn), lambda j, i: (i, j)),
        out_shape=jax.ShapeDtypeStruct((m, n), F32),
        scratch_shapes=[pltpu.VMEM((k, tn), BF16), pltpu.VMEM((k, tn), BF16)],
        compiler_params=_params(("arbitrary", "arbitrary"), 48),
        name="pw1_glu",
    )(xb, w, w, b2, b2)


def _dwconv_ln_silu_kernel(h_ref, halo_ref, w_ref, b_ref, g_ref, beta_ref, o_ref, buf, cbuf,
                           *, ts, width, cw, rb):
    i = pl.program_id(0)
    d = h_ref.shape[1]
    buf[0:HALO_ROWS, :] = jnp.where(i > 0, halo_ref[...], 0.0)
    buf[HALO_ROWS:HALO_ROWS + ts, :] = h_ref[...]

    def col_body(c, carry):
        cols = pl.ds(pl.multiple_of(c * cw, cw), cw)
        for r in range(ts // rb):
            acc = jnp.broadcast_to(b_ref[:, cols], (rb, cw))
            for b in range(SUBLANES):
                yb = None
                for a in range((width - 1 - b) // SUBLANES + 1):
                    j = width - 1 - (SUBLANES * a + b)
                    start = r * rb + HALO_ROWS - SUBLANES * (a + 1)
                    term = w_ref[j:j + 1, cols] * buf[pl.ds(start, rb + SUBLANES), cols]
                    yb = term if yb is None else yb + term
                acc = acc + yb[SUBLANES - b:SUBLANES - b + rb]
            cbuf[pl.ds(r * rb, rb), cols] = acc
        return carry

    lax.fori_loop(0, d // cw, col_body, 0)
    y = _layer_norm(cbuf[...], g_ref[...], beta_ref[...])
    o_ref[...] = _silu(y).astype(o_ref.dtype)


def dwconv_ln_silu(h, w_dw, b_dw, ln_g, ln_b, *, ts=256, cw=128, rb=128):
    t, d = h.shape
    width = w_dw.shape[0]
    assert SUBLANES * ((width - 1) // SUBLANES + 1) <= HALO_ROWS
    ts, cw = min(ts, t), min(cw, d)
    halo_per_tile = ts // HALO_ROWS
    kern = functools.partial(_dwconv_ln_silu_kernel, ts=ts, width=width, cw=cw, rb=rb)
    row = lambda a: a.reshape(1, d)
    return pl.pallas_call(
        kern,
        grid=(t // ts,),
        in_specs=[
            pl.BlockSpec((ts, d), lambda i: (i, 0)),
            pl.BlockSpec((HALO_ROWS, d), lambda i: (jnp.maximum(i * halo_per_tile - 1, 0), 0)),
            pl.BlockSpec((width, d), lambda i: (0, 0)),
            pl.BlockSpec((1, d), lambda i: (0, 0)),
            pl.BlockSpec((1, d), lambda i: (0, 0)),
            pl.BlockSpec((1, d), lambda i: (0, 0)),
        ],
        out_specs=pl.BlockSpec((ts, d), lambda i: (i, 0)),
        out_shape=jax.ShapeDtypeStruct((t, d), BF16),
        scratch_shapes=[pltpu.VMEM((ts + HALO_ROWS, d), F32), pltpu.VMEM((ts, d), F32)],
        compiler_params=_params(("arbitrary",), 40),
        name="dwconv_ln_silu",
    )(h, h, w_dw, row(b_dw), row(ln_g), row(ln_b))


def _mm_res_ln_kernel(a_ref, w_hbm, bias_ref, res_ref, g_ref, b_ref, o_ref, ob_ref, wres, stage,
                      sem, *, layer, tk, sub):
    nk = wres.shape[0] // tk

    @pl.when(pl.program_id(0) == 0)
    def _():
        def chunk_copy(c):
            return pltpu.make_async_copy(w_hbm.at[layer, pl.ds(c * tk, tk)], stage.at[c % 2],
                                         sem.at[c % 2])
        chunk_copy(0).start()
        for c in range(nk):
            if c + 1 < nk:
                chunk_copy(c + 1).start()
            chunk_copy(c).wait()
            wres[c * tk:(c + 1) * tk, :] = stage[c % 2].astype(BF16)

    for sb in range(a_ref.shape[0] // sub):
        rows = slice(sb * sub, (sb + 1) * sub)
        y = DEEPNORM_ALPHA * res_ref[rows, :] + (_dot(a_ref[rows, :], wres[...]) + bias_ref[...])
        out = _layer_norm(y, g_ref[...], b_ref[...])
        o_ref[rows, :] = out
        ob_ref[rows, :] = out.astype(BF16)


def mm_res_ln(a, w_all, layer, bias, res, ln_g, ln_b, *, tm=256, tk=512, sub=128):
    m, k = a.shape
    n = w_all.shape[2]
    tm, tk = min(tm, m), min(tk, k)
    row = lambda v: v.reshape(1, n)
    return pl.pallas_call(
        functools.partial(_mm_res_ln_kernel, layer=layer, tk=tk, sub=min(sub, tm)),
        grid=(m // tm,),
        in_specs=[
            pl.BlockSpec((tm, k), lambda i: (i, 0)),
            pl.BlockSpec(memory_space=pl.ANY),
            pl.BlockSpec((1, n), lambda i: (0, 0)),
            pl.BlockSpec((tm, n), lambda i: (i, 0)),
            pl.BlockSpec((1, n), lambda i: (0, 0)),
            pl.BlockSpec((1, n), lambda i: (0, 0)),
        ],
        out_specs=[pl.BlockSpec((tm, n), lambda i: (i, 0)),
                   pl.BlockSpec((tm, n), lambda i: (i, 0))],
        out_shape=[jax.ShapeDtypeStruct((m, n), F32), jax.ShapeDtypeStruct((m, n), BF16)],
        scratch_shapes=[pltpu.VMEM((k, n), BF16), pltpu.VMEM((2, tk, n), F32),
                        pltpu.SemaphoreType.DMA((2,))],
        compiler_params=_params(("arbitrary",), 48),
        name="mm_res_ln",
    )(a, w_all, row(bias), res, row(ln_g), row(ln_b))


def _gdn_in_kernel(x_ref, w_ref, wc_ref, o_ref, wbf, pbuf, *, tm, tn, sub, nq, nqk, nconv, cwidth):
    j = pl.program_id(0)
    i = pl.program_id(1)
    heads = [slice(hh * HEAD_DIM, (hh + 1) * HEAD_DIM) for hh in range(tn // HEAD_DIM)]

    @pl.when(i == 0)
    def _():
        wbf[...] = w_ref[...].astype(BF16)
        pbuf[0:CARRY_ROWS, :] = jnp.zeros((CARRY_ROWS, tn), F32)

    def for_sub_blocks(epilogue):
        for sb in range(tm // sub):
            rows = slice(sb * sub, (sb + 1) * sub)
            epilogue(sb, rows, _dot_nt(x_ref[rows, :], wbf[...]))

    def conv_silu(sb, p):
        base = CARRY_ROWS + sb * sub
        pbuf[base:base + sub, :] = p
        c = None
        for jj in range(cwidth):
            term = wc_ref[jj:jj + 1, :] * pbuf[pl.ds(base - (cwidth - 1) + jj, sub), :]
            c = term if c is None else c + term
        return _silu(c)

    def store_heads(rows, s):
        for hh, cols in enumerate(heads):
            o_ref[hh, rows, :] = s[:, cols]

    def gate_epilogue(sb, rows, p):
        store_heads(rows, _silu(p))

    def value_epilogue(sb, rows, p):
        store_heads(rows, conv_silu(sb, p))

    def qk_epilogue(sb, rows, p):
        s = conv_silu(sb, p)
        scale = jnp.where(j < nq, HEAD_DIM ** -0.5, 1.0).astype(F32)
        for hh, cols in enumerate(heads):
            blk = s[:, cols]
            ss = jnp.sum(blk * blk, axis=-1, keepdims=True)
            o_ref[hh, rows, :] = blk * lax.rsqrt(ss + NORM_EPS) * scale

    @pl.when(j >= nconv)
    def _():
        for_sub_blocks(gate_epilogue)

    @pl.when((j >= nqk) & (j < nconv))
    def _():
        for_sub_blocks(value_epilogue)
        pbuf[0:CARRY_ROWS, :] = pbuf[tm:tm + CARRY_ROWS, :]

    @pl.when(j < nqk)
    def _():
        for_sub_blocks(qk_epilogue)
        pbuf[0:CARRY_ROWS, :] = pbuf[tm:tm + CARRY_ROWS, :]


def gdn_in_proj(xb, w_t_all, layer, w_conv, *, tm=1024, tn=512, sub=256):
    t, d = xb.shape
    qk_dim = d
    v_dim = 2 * d
    n_main = 2 * qk_dim + 2 * v_dim
    tm, tn = min(tm, t), min(tn, qk_dim)
    cwidth = w_conv.shape[0]
    nq = qk_dim // tn
    nconv = (2 * qk_dim + v_dim) // tn
    hpt = tn // HEAD_DIM
    kern = functools.partial(_gdn_in_kernel, tm=tm, tn=tn, sub=min(sub, tm), nq=nq, nqk=2 * nq,
                             nconv=nconv, cwidth=cwidth)
    return pl.pallas_call(
        kern,
        grid=(n_main // tn, t // tm),
        in_specs=[
            pl.BlockSpec((tm, d), lambda j, i: (i, 0)),
            pl.BlockSpec((None, tn, d), lambda j, i: (layer, j, 0)),
            pl.BlockSpec((cwidth, tn), lambda j, i: (0, jnp.minimum(j, nconv - 1))),
        ],
        out_specs=pl.BlockSpec((hpt, tm, HEAD_DIM), lambda j, i: (j, i, 0)),
        out_shape=jax.ShapeDtypeStruct((n_main // HEAD_DIM, t, HEAD_DIM), F32),
        scratch_shapes=[pltpu.VMEM((tn, d), BF16), pltpu.VMEM((tm + CARRY_ROWS, tn), F32)],
        compiler_params=_params(("arbitrary", "arbitrary"), 48),
        name="gdn_in_proj",
    )(xb, w_t_all, w_conv)


def _gdn_gate_kernel(x_ref, wt_ref, alog_ref, dtb_ref, col_ref, gct_ref, *, tm, hg):
    nh = alog_ref.shape[1]
    xh, xl = _split_bf16(x_ref[...])
    wh, wl = _split_bf16(wt_ref[...])
    proj = _dot_nt(xh, wh) + _dot_nt(xl, wh) + _dot_nt(xh, wl)
    beta = _sigmoid(proj[:, :nh])
    a = proj[:, nh:] + dtb_ref[...]
    softplus = jnp.maximum(a, 0.0) + jnp.log(1.0 + jnp.exp(-jnp.abs(a)))
    g = -jnp.exp(alog_ref[...]) * softplus
    r = lax.broadcasted_iota(jnp.int32, (tm, tm), 0)
    c = lax.broadcasted_iota(jnp.int32, (tm, tm), 1)
    same = r // GDN_CHUNK == c // GDN_CHUNK
    lower = jnp.where(same & (c <= r), 1.0, 0.0).astype(BF16)
    upper = jnp.where(same & (r <= c), 1.0, 0.0).astype(BF16)
    g1 = g.astype(BF16)
    r1 = g - g1.astype(F32)
    g2 = r1.astype(BF16)
    g3 = (r1 - g2.astype(F32)).astype(BF16)
    gc = _dot(lower, g1) + _dot(lower, g2) + _dot(lower, g3)
    gct_ref[...] = _dot_tn(g1, upper) + _dot_tn(g2, upper) + _dot_tn(g3, upper)
    for grp in range(col_ref.shape[0]):
        cols = slice(grp * hg, (grp + 1) * hg)
        col_ref[grp] = jnp.concatenate([beta[:, cols], gc[:, cols]], axis=1)


def gdn_gates(x, w_t_all, layer, row0, a_log, dt_bias, *, hg, tm=512):
    t, d = x.shape
    h = a_log.shape[0]
    tm = min(tm, t)
    row = lambda v: v.reshape(1, h)
    return pl.pallas_call(
        functools.partial(_gdn_gate_kernel, tm=tm, hg=hg),
        grid=(t // tm,),
        in_specs=[
            pl.BlockSpec((tm, d), lambda i: (i, 0)),
            pl.BlockSpec((None, 2 * h, d), lambda i: (layer, row0 // (2 * h), 0)),
            pl.BlockSpec((1, h), lambda i: (0, 0)),
            pl.BlockSpec((1, h), lambda i: (0, 0)),
        ],
        out_specs=[pl.BlockSpec((h // hg, tm, 2 * hg), lambda i: (0, i, 0)),
                   pl.BlockSpec((h, tm), lambda i: (0, i))],
        out_shape=[jax.ShapeDtypeStruct((h // hg, t, 2 * hg), F32),
                   jax.ShapeDtypeStruct((h, t), F32)],
        compiler_params=_params(("arbitrary",), 32),
        name="gdn_gates",
    )(x, w_t_all, row(a_log), row(dt_bias))


def _gdn_chunk_kernel(q_ref, k_ref, v_ref, z_ref, col_ref, row_ref, nw_ref, o_ref, state,
                      *, nc, kg):
    c_ = GDN_CHUNK
    hd = HEAD_DIM
    vh = 2 * kg

    @pl.when(pl.program_id(1) == 0)
    def _():
        state[...] = jnp.zeros(state.shape, F32)

    ii = lax.broadcasted_iota(jnp.int32, (c_, c_), 0)
    jj = lax.broadcasted_iota(jnp.int32, (c_, c_), 1)
    tril = ii >= jj
    strict = ii > jj

    q = q_ref[...].reshape(kg * nc, c_, hd)
    k = k_ref[...].reshape(kg * nc, c_, hd)
    kq = lax.dot_general(jnp.concatenate([k, q], axis=1).astype(BF16), k.astype(BF16),
                         (((2,), (2,)), ((0,), (0,))), preferred_element_type=F32)
    col = col_ref[...]
    n_steps = c_.bit_length() - 1
    bmm = lambda a, b: jnp.einsum("bij,bjk->bik", a, b, preferred_element_type=F32)
    row2 = lax.broadcasted_iota(jnp.int32, (c_, 2 * c_), 0)
    lane2 = lax.broadcasted_iota(jnp.int32, (c_, 2 * c_), 1)
    lane_lo = lane2 < c_
    eye2 = jnp.where((row2 == lane2) | (row2 + c_ == lane2), 1.0, 0.0)

    def block_diag(packed):
        return jnp.concatenate([jnp.where(lane_lo, packed, 0.0), jnp.where(lane_lo, 0.0, packed)],
                               axis=1).astype(BF16)

    def solve_heads(heads):
        neg_a, rhs, qk, k_dec, q_dec, g_last = [], [], [], [], [], []
        for h in heads:
            own = slice((h // 2) * nc, (h // 2 + 1) * nc)
            bcol = jnp.broadcast_to(col[:, h:h + 1], (nc * c_, hd)).reshape(nc, c_, hd)
            gcol = jnp.broadcast_to(col[:, vh + h:vh + h + 1], (nc * c_, hd)).reshape(nc, c_, hd)
            grow = row_ref[h]
            decay = jnp.where(tril, jnp.exp(jnp.minimum(gcol[:, :, :c_] - grow, 0.0)), 0.0)
            neg_a.append(jnp.where(strict, -(bcol[:, :, :c_] * kq[own, :c_] * decay), 0.0))
            qk.append((kq[own, c_:] * decay).astype(BF16))
            eg = jnp.exp(gcol)
            glast = gcol[:, c_ - 1:c_, :]
            rhs.append(jnp.concatenate([v_ref[h].reshape(nc, c_, hd) * bcol, k[own] * (bcol * eg)],
                                       axis=2))
            k_dec.append((k[own] * jnp.exp(glast - gcol)).astype(BF16))
            q_dec.append(q[own] * eg)
            g_last.append(jnp.exp(glast))

        def chunk_major(per_head):
            stacked = jnp.stack(per_head, axis=1)
            return stacked.reshape((nc * len(heads),) + stacked.shape[2:])

        pw = chunk_major(neg_a)
        rhs_all = chunk_major(rhs)
        half = (nc * len(heads)) // 2
        q_pow = jnp.concatenate([pw[:half], pw[half:]], axis=2)
        t_inv = eye2 + q_pow
        for s in range(1, n_steps):
            q_pow = bmm(q_pow.astype(BF16), block_diag(q_pow))
            t_inv = t_inv + bmm(q_pow.astype(BF16), block_diag(t_inv))
        stacked_rhs = jnp.concatenate([rhs_all[:half], rhs_all[half:]], axis=1)
        corr = bmm(block_diag(t_inv - eye2), stacked_rhs.astype(BF16))
        sol = rhs_all + jnp.concatenate([corr[:, :c_], corr[:, c_:]], axis=0)
        wq = jnp.concatenate([sol[:, :, hd:], chunk_major(q_dec)], axis=1).astype(BF16)
        return sol[:, :, :hd], wq, chunk_major(qk), chunk_major(k_dec), chunk_major(g_last)

    gh = vh // GDN_SOLVE_GROUPS
    solved = [solve_heads(range(g * gh, (g + 1) * gh)) for g in range(GDN_SOLVE_GROUPS)]

    def per_chain(field, c, h):
        return solved[h // gh][field][c * gh + h % gh]

    nw = nw_ref[...]
    s_cur = [state[h] for h in range(vh)]
    for c in range(nc):
        rows = slice(c * c_, (c + 1) * c_)
        ws = [_dot(per_chain(1, c, h), s_cur[h].astype(BF16)) for h in range(vh)]
        v_new = [(per_chain(0, c, h) - ws[h][:c_]).astype(BF16) for h in range(vh)]
        o = [ws[h][c_:] + _dot(per_chain(2, c, h), v_new[h]) for h in range(vh)]
        s_cur = [s_cur[h] * per_chain(4, c, h) + _dot_tn(per_chain(3, c, h), v_new[h])
                 for h in range(vh)]
        for h in range(vh):
            on = o[h] * lax.rsqrt(jnp.mean(o[h] * o[h], axis=-1, keepdims=True) + NORM_EPS) * nw
            o_ref[rows, h * hd:(h + 1) * hd] = (on * z_ref[h, rows, :]).astype(o_ref.dtype)
    for h in range(vh):
        state[h] = s_cur[h]


def gdn_chunk(proj, col, gct, norm_w, *, kg, tb=512):
    t = proj.shape[1]
    nkh = proj.shape[0] // 6
    vh = 2 * kg
    tb = min(tb, t)
    nc = tb // GDN_CHUNK
    n_chunks = t // GDN_CHUNK
    row = gct.reshape(nkh // kg, vh, n_chunks, 1, GDN_CHUNK)
    return pl.pallas_call(
        functools.partial(_gdn_chunk_kernel, nc=nc, kg=kg),
        grid=(nkh // kg, t // tb),
        in_specs=[
            pl.BlockSpec((kg, tb, HEAD_DIM), lambda g, n: (g, n, 0)),
            pl.BlockSpec((kg, tb, HEAD_DIM), lambda g, n: (nkh // kg + g, n, 0)),
            pl.BlockSpec((vh, tb, HEAD_DIM), lambda g, n: (2 * nkh // vh + g, n, 0)),
            pl.BlockSpec((vh, tb, HEAD_DIM), lambda g, n: (4 * nkh // vh + g, n, 0)),
            pl.BlockSpec((None, tb, 2 * vh), lambda g, n: (g, n, 0)),
            pl.BlockSpec((None, vh, nc, 1, GDN_CHUNK), lambda g, n: (g, 0, n, 0, 0)),
            pl.BlockSpec((1, HEAD_DIM), lambda g, n: (0, 0)),
        ],
        out_specs=pl.BlockSpec((tb, vh * HEAD_DIM), lambda g, n: (n, g)),
        out_shape=jax.ShapeDtypeStruct((t, 2 * nkh * HEAD_DIM), BF16),
        scratch_shapes=[pltpu.VMEM((vh, HEAD_DIM, HEAD_DIM), F32)],
        compiler_params=_params(("arbitrary", "arbitrary"), 48),
        name="gdn_chunk",
    )(proj, proj, proj, proj, col, row, norm_w.reshape(1, HEAD_DIM))


def _top2_of_4(a, b, c, d):
    hi1, lo1 = jnp.maximum(a, b), jnp.minimum(a, b)
    hi2, lo2 = jnp.maximum(c, d), jnp.minimum(c, d)
    return jnp.maximum(hi1, hi2), jnp.maximum(jnp.minimum(hi1, hi2), jnp.maximum(lo1, lo2))


def _first_argmax(vals):
    best, arg = vals[0], jnp.zeros(vals[0].shape, jnp.int32)
    for n in range(1, len(vals)):
        upd = vals[n] > best
        arg = jnp.where(upd, n, arg)
        best = jnp.where(upd, vals[n], best)
    return best, arg


def _router_kernel(x_ref, w_ref, b_ref, idx_ref, gate_ref, rank_ref, cnt_ref, run, *, tm):
    @pl.when(pl.program_id(0) == 0)
    def _():
        run[...] = jnp.zeros(run.shape, F32)

    xh, xl = _split_bf16(x_ref[...])
    wh, wl = _split_bf16(w_ref[...])
    logits = _dot_nt(wh, xh) + _dot_nt(wh, xl) + _dot_nt(wl, xh) + b_ref[...]
    e = jnp.exp(logits - jnp.max(logits, axis=0, keepdims=True))
    p = e / jnp.sum(e, axis=0, keepdims=True)
    rows = [p[r:r + 1, :] for r in range(N_EXPERTS)]
    scores = []
    for g in range(N_GROUPS):
        t1, t2 = _top2_of_4(*rows[g * EXPERTS_PER_GROUP:(g + 1) * EXPERTS_PER_GROUP])
        scores.append(t1 + t2)
    _, grp = _first_argmax(scores)
    vals = []
    for j in range(EXPERTS_PER_GROUP):
        v = rows[(N_GROUPS - 1) * EXPERTS_PER_GROUP + j]
        for g in range(N_GROUPS - 2, -1, -1):
            v = jnp.where(grp == g, rows[g * EXPERTS_PER_GROUP + j], v)
        vals.append(v)
    v1, i1 = _first_argmax(vals)
    v2, i2 = _first_argmax([jnp.where(i1 == j, -1.0, vals[j]) for j in range(EXPERTS_PER_GROUP)])
    e1 = grp * EXPERTS_PER_GROUP + i1
    e2 = grp * EXPERTS_PER_GROUP + i2
    den = v1 + v2
    idx_ref[0:1, :] = e1
    idx_ref[1:2, :] = e2
    gate_ref[0:1, :] = v1 / den
    gate_ref[1:2, :] = v2 / den

    eio = lax.broadcasted_iota(jnp.int32, (N_EXPERTS, tm), 0)
    hit1 = eio == e1
    hit2 = eio == e2
    onehot = jnp.where(hit1 | hit2, 1.0, 0.0)
    r = lax.broadcasted_iota(jnp.int32, (tm, tm), 0)
    c = lax.broadcasted_iota(jnp.int32, (tm, tm), 1)
    before = jnp.where(r < c, 1.0, 0.0).astype(BF16)
    prefix = _dot(onehot.astype(BF16), before) + run[:, 0:1]
    rank_ref[0:1, :] = jnp.sum(jnp.where(hit1, prefix, 0.0), axis=0, keepdims=True).astype(jnp.int32)
    rank_ref[1:2, :] = jnp.sum(jnp.where(hit2, prefix, 0.0), axis=0, keepdims=True).astype(jnp.int32)
    run[...] = run[...] + jnp.sum(onehot, axis=1, keepdims=True)
    cnt_ref[...] = run[...].astype(jnp.int32)


def moe_router(x, router_w, router_b, *, tm=512):
    t, d = x.shape
    tm = min(tm, t)
    out2 = lambda dt: jax.ShapeDtypeStruct((TOP_K, t), dt)
    blk2 = pl.BlockSpec((TOP_K, tm), lambda i: (0, i))
    return pl.pallas_call(
        functools.partial(_router_kernel, tm=tm),
        grid=(t // tm,),
        in_specs=[
            pl.BlockSpec((tm, d), lambda i: (i, 0)),
            pl.BlockSpec((N_EXPERTS, d), lambda i: (0, 0)),
            pl.BlockSpec((N_EXPERTS, 1), lambda i: (0, 0)),
        ],
        out_specs=[blk2, blk2, blk2, pl.BlockSpec((N_EXPERTS, 128), lambda i: (0, 0))],
        out_shape=[out2(jnp.int32), out2(F32), out2(jnp.int32),
                   jax.ShapeDtypeStruct((N_EXPERTS, 128), jnp.int32)],
        scratch_shapes=[pltpu.VMEM((N_EXPERTS, 128), F32)],
        compiler_params=_params(("arbitrary",), 32),
        name="moe_router",
    )(x, router_w.T, router_b.reshape(N_EXPERTS, 1))


def _dispatch_kernel(dest_ref, pend_ref, x_ref, xs_hbm, zbuf, sem, zsem, *, td, t, tm):
    base = pl.program_id(0) * td

    def zero_tile_copy(start):
        return pltpu.make_async_copy(zbuf, xs_hbm.at[pl.ds(pl.multiple_of(start, tm), tm)], zsem)

    @pl.when(pl.program_id(0) == 0)
    def _():
        zbuf[...] = jnp.zeros(zbuf.shape, zbuf.dtype)
        n_rows = xs_hbm.shape[0]
        starts, valid = [], []
        for e in range(N_EXPERTS):
            starts.append(pend_ref[e] - tm)
            valid.append(pend_ref[e] > (pend_ref[e - 1] if e else 0))
            starts.append(pend_ref[N_EXPERTS - 1] + e * tm)
            valid.append(starts[-1] < n_rows)
        for start, ok in zip(starts, valid):
            @pl.when(ok)
            def _():
                zero_tile_copy(start).start()
        for start, ok in zip(starts, valid):
            @pl.when(ok)
            def _():
                zero_tile_copy(start).wait()

    def row_copy(src_row, dst_row):
        return pltpu.make_async_copy(x_ref.at[pl.ds(src_row, 1)], xs_hbm.at[pl.ds(dst_row, 1)], sem)

    def issue(n, carry):
        for k in range(TOP_K):
            row_copy(n, dest_ref[k * t + base + n]).start()
        return carry

    lax.fori_loop(0, td, issue, 0, unroll=DMA_ISSUE_UNROLL)
    all_rows = xs_hbm.at[pl.ds(0, TOP_K * td)]
    pltpu.make_async_copy(all_rows, all_rows, sem).wait()


def moe_dispatch(x, dest_flat, pend, n_rows, *, tm, td=512):
    t, d = x.shape
    td = min(td, t)
    return pl.pallas_call(
        functools.partial(_dispatch_kernel, td=td, t=t, tm=tm),
        grid_spec=pltpu.PrefetchScalarGridSpec(
            num_scalar_prefetch=2,
            grid=(t // td,),
            in_specs=[pl.BlockSpec((td, d), lambda i, dest, pend: (i, 0))],
            out_specs=pl.BlockSpec(memory_space=pl.ANY),
            scratch_shapes=[pltpu.VMEM((tm, d), x.dtype), pltpu.SemaphoreType.DMA(()),
                            pltpu.SemaphoreType.DMA(())],
        ),
        out_shape=jax.ShapeDtypeStruct((n_rows, d), x.dtype),
        compiler_params=_params(("arbitrary",), 32),
        name="moe_dispatch",
    )(dest_flat, pend, x)


def _moe_ffn_kernel(te_ref, first_ref, next_ref, nu_ref, x_ref, wg_hbm, wu_hbm, wd_hbm, y_ref,
                    wg_res, wu_res, wd_res, wg_st, wu_st, wd_st, sem, *, layer, tf):
    i = pl.program_id(0)
    n_chunks = wg_res.shape[1] // tf
    n_stage = wg_st.shape[0]
    chunk = lambda c: slice(c * tf, (c + 1) * tf)

    def chunk_copies(expert, c):
        slot = c % n_stage
        return (
            pltpu.make_async_copy(wg_hbm.at[layer, expert, :, chunk(c)], wg_st.at[slot], sem.at[0, slot]),
            pltpu.make_async_copy(wu_hbm.at[layer, expert, :, chunk(c)], wu_st.at[slot], sem.at[1, slot]),
            pltpu.make_async_copy(wd_hbm.at[layer, expert, chunk(c), :], wd_st.at[slot], sem.at[2, slot]),
        )

    def start_chunk(expert, c):
        for copy in chunk_copies(expert, c):
            copy.start()

    def land_chunk(expert, c):
        for copy in chunk_copies(expert, c):
            copy.wait()
        wg_res[:, chunk(c)] = wg_st[c % n_stage].astype(BF16)
        wu_res[:, chunk(c)] = wu_st[c % n_stage].astype(BF16)
        wd_res[chunk(c), :] = wd_st[c % n_stage].astype(BF16)

    def ffn_chunk(x, c):
        h = _silu(_dot(x, wg_res[:, chunk(c)])) * _dot(x, wu_res[:, chunk(c)])
        return _dot(h.astype(BF16), wd_res[chunk(c), :])

    active = i < nu_ref[0]
    is_first = first_ref[i] == 1

    @pl.when(jnp.logical_not(active))
    def _():
        y_ref[...] = jnp.zeros(y_ref.shape, F32)

    @pl.when(i == 0)
    def _():
        for c in range(min(n_stage, n_chunks)):
            start_chunk(te_ref[0], c)

    @pl.when(active & is_first)
    def _():
        expert = te_ref[i]
        x = x_ref[...].astype(BF16)
        acc = None
        for c in range(n_chunks):
            land_chunk(expert, c)
            if c + n_stage < n_chunks:
                start_chunk(expert, c + n_stage)
            part = ffn_chunk(x, c)
            acc = part if acc is None else acc + part
        y_ref[...] = acc

        @pl.when(next_ref[i] >= 0)
        def _():
            for c in range(min(n_stage, n_chunks)):
                start_chunk(next_ref[i], c)

    @pl.when(active & jnp.logical_not(is_first))
    def _():
        x = x_ref[...].astype(BF16)
        acc = None
        for c in range(n_chunks):
            part = ffn_chunk(x, c)
            acc = part if acc is None else acc + part
        y_ref[...] = acc


def moe_ffn(xs, tile_expert, tile_first, tile_next, n_used, wg, wu, wd, layer, *, tm, tf=256):
    n_rows, d = xs.shape
    f_dim = wg.shape[3]
    tf = math.gcd(tf, f_dim)
    row_map = lambda i, te, first, nxt, nu: (jnp.minimum(i, nu[0] - 1), 0)
    any_spec = pl.BlockSpec(memory_space=pl.ANY)
    return pl.pallas_call(
        functools.partial(_moe_ffn_kernel, layer=layer, tf=tf),
        grid_spec=pltpu.PrefetchScalarGridSpec(
            num_scalar_prefetch=4,
            grid=(n_rows // tm,),
            in_specs=[pl.BlockSpec((tm, d), row_map), any_spec, any_spec, any_spec],
            out_specs=pl.BlockSpec((tm, d), lambda i, te, first, nxt, nu: (i, 0)),
            scratch_shapes=[
                pltpu.VMEM((d, f_dim), BF16), pltpu.VMEM((d, f_dim), BF16), pltpu.VMEM((f_dim, d), BF16),
                pltpu.VMEM((MOE_WEIGHT_STAGES, d, tf), F32), pltpu.VMEM((MOE_WEIGHT_STAGES, d, tf), F32),
                pltpu.VMEM((MOE_WEIGHT_STAGES, tf, d), F32),
                pltpu.SemaphoreType.DMA((3, MOE_WEIGHT_STAGES)),
            ],
        ),
        out_shape=jax.ShapeDtypeStruct((n_rows, d), F32),
        compiler_params=_params(("arbitrary",), 60),
        name="moe_ffn",
    )(tile_expert, tile_first, tile_next, n_used, xs, wg, wu, wd)


def _combine_kernel(dest_ref, y_hbm, x_ref, gt_ref, g_ref, b_ref, o_ref, ob_ref, buf, sem,
                    *, tc, n_tiles, t):
    i = pl.program_id(0)

    def row_copy(src_row, slot, k, n):
        return pltpu.make_async_copy(y_hbm.at[pl.ds(src_row, 1)], buf.at[slot, k, pl.ds(n, 1)],
                                     sem.at[slot])

    @pl.when(i == 0)
    def _():
        def body(n, carry):
            for k in range(TOP_K):
                row_copy(dest_ref[k * t + n], 0, k, n).start()
            return carry
        lax.fori_loop(0, tc, body, 0, unroll=DMA_ISSUE_UNROLL)

    def wait_rows(slot):
        pltpu.make_async_copy(buf.at[slot], buf.at[slot], sem.at[slot]).wait()

    def finish(slot):
        gt = gt_ref[...]
        mix = gt[:, 0:1] * buf[slot, 0] + gt[:, 1:2] * buf[slot, 1]
        out = _layer_norm(DEEPNORM_ALPHA * x_ref[...] + mix, g_ref[...], b_ref[...])
        o_ref[...] = out
        ob_ref[...] = out.astype(BF16)

    def gather_next_and_finish(slot):
        wait_rows(slot)
        base = (i + 1) * tc
        for n in range(tc):
            for k in range(TOP_K):
                row_copy(dest_ref[k * t + base + n], 1 - slot, k, n).start()
        finish(slot)

    has_next = i + 1 < n_tiles

    @pl.when(has_next & (i % 2 == 0))
    def _():
        gather_next_and_finish(0)

    @pl.when(has_next & (i % 2 == 1))
    def _():
        gather_next_and_finish(1)

    @pl.when(jnp.logical_not(has_next))
    def _():
        wait_rows((n_tiles - 1) % 2)
        finish((n_tiles - 1) % 2)


def moe_combine(y, dest_flat, x, gates_t, ln_g, ln_b, *, tc=256):
    t, d = x.shape
    tc = min(tc, t)
    n_tiles = t // tc
    row = lambda v: v.reshape(1, d)
    blk = lambda i, dest: (i, 0)
    fixed = lambda i, dest: (0, 0)
    return pl.pallas_call(
        functools.partial(_combine_kernel, tc=tc, n_tiles=n_tiles, t=t),
        grid_spec=pltpu.PrefetchScalarGridSpec(
            num_scalar_prefetch=1,
            grid=(n_tiles,),
            in_specs=[
                pl.BlockSpec(memory_space=pl.ANY),
                pl.BlockSpec((tc, d), blk),
                pl.BlockSpec((tc, TOP_K), blk),
                pl.BlockSpec((1, d), fixed),
                pl.BlockSpec((1, d), fixed),
            ],
            out_specs=[pl.BlockSpec((tc, d), blk), pl.BlockSpec((tc, d), blk)],
            scratch_shapes=[pltpu.VMEM((2, TOP_K, tc, d), F32), pltpu.SemaphoreType.DMA((2,))],
        ),
        out_shape=[jax.ShapeDtypeStruct((t, d), F32), jax.ShapeDtypeStruct((t, d), BF16)],
        compiler_params=_params(("arbitrary",), 40),
        name="moe_combine",
    )(dest_flat, y, x, gates_t, row(ln_g), row(ln_b))


def moe_layer(x, router_w, router_b, wg, wu, wd, layer, ln_g, ln_b, *, tm=256):
    t, d = x.shape
    idx, gates, rank, cnt = moe_router(x, router_w, router_b)
    counts = cnt[:, 0]
    padded = (counts + tm - 1) // tm * tm
    pend = jnp.cumsum(padded).astype(jnp.int32)
    pstart = pend - padded
    experts = jnp.arange(N_EXPERTS, dtype=jnp.int32)
    seg_start = jnp.sum(jnp.where(idx[:, :, None] == experts, pstart, 0), axis=-1)
    dest = (seg_start + rank).reshape(TOP_K * t).astype(jnp.int32)
    n_tiles = (t * TOP_K) // tm + N_EXPERTS
    n_used = pend[-1] // tm
    tile_start = jnp.arange(n_tiles, dtype=jnp.int32) * tm
    te = jnp.sum((pend[None, :] <= tile_start[:, None]).astype(jnp.int32), axis=1)
    last_used = jnp.sum((pend <= (n_used - 1) * tm).astype(jnp.int32))
    te = jnp.where(tile_start < pend[-1], te, last_used).astype(jnp.int32)
    first = jnp.concatenate([jnp.ones((1,), jnp.int32), (te[1:] != te[:-1]).astype(jnp.int32)])
    later_nonempty = (counts[None, :] > 0) & (experts[None, :] > experts[:, None])
    next_expert = jnp.min(jnp.where(later_nonempty, experts[None, :], N_EXPERTS), axis=1)
    next_expert = jnp.where(next_expert < N_EXPERTS, next_expert, -1)
    tile_next = jnp.sum(jnp.where(te[:, None] == experts, next_expert, 0), axis=1).astype(jnp.int32)

    xs = moe_dispatch(x, dest, pend, n_tiles * tm, tm=tm)
    y = moe_ffn(xs, te, first, tile_next, n_used.reshape(1), wg, wu, wd, layer, tm=tm)
    return moe_combine(y, dest, x, gates.T, ln_g, ln_b)


def kernel(x, conv_w_pw1, conv_b_pw1, conv_w_dw, conv_b_dw, conv_ln_g, conv_ln_b, conv_w_pw2,
           conv_b_pw2, gdn_w_in, gdn_w_conv, gdn_a_log, gdn_dt_bias, gdn_norm_w, gdn_w_out,
           router_w, router_b, moe_w_gate, moe_w_up, moe_w_down, ln_mix_g, ln_mix_b, ln_ffn_g,
           ln_ffn_b):
    b_, s_, d = x.shape
    t = b_ * s_
    xf = x.reshape(t, d)
    xb = xf.astype(BF16)
    n_main = 6 * d
    n_vh = gdn_a_log.shape[1]
    kg = min(GDN_KEY_HEADS_PER_STEP, n_vh // 2)
    w_in_t = jnp.swapaxes(gdn_w_in, 1, 2)
    for i in range(DEPTH):
        j = i // 2
        if i % 2 == 0:
            h = pw1_glu(xb, conv_w_pw1, conv_b_pw1[j], j)
            hb = dwconv_ln_silu(h, conv_w_dw[j], conv_b_dw[j], conv_ln_g[j], conv_ln_b[j])
            xf, xb = mm_res_ln(hb, conv_w_pw2, j, conv_b_pw2[j], xf, ln_mix_g[i], ln_mix_b[i])
        else:
            proj = gdn_in_proj(xb, w_in_t, j, gdn_w_conv[j])
            col, gct = gdn_gates(xf, w_in_t, j, n_main, gdn_a_log[j], gdn_dt_bias[j], hg=2 * kg)
            ob = gdn_chunk(proj, col, gct, gdn_norm_w[j], kg=kg)
            xf, xb = mm_res_ln(ob, gdn_w_out, j, jnp.zeros((d,), F32), xf, ln_mix_g[i], ln_mix_b[i])
        xf, xb = moe_layer(xf, router_w, router_b, moe_w_gate, moe_w_up, moe_w_down, i,
                           ln_ffn_g[i], ln_ffn_b[i])
    return xf.reshape(b_, s_, d)
```

```python
import functools
import math

import jax
import jax.numpy as jnp
from jax import lax
from jax.experimental import pallas as pl
from jax.experimental.pallas import tpu as pltpu

F32 = jnp.float32
BF16 = jnp.bfloat16

DEPTH = 4
DEEPNORM_ALPHA = (2 * DEPTH) ** 0.25
LN_EPS = 1e-5
NORM_EPS = 1e-6
HEAD_DIM = 128
GDN_CHUNK = 64
N_EXPERTS = 16
N_GROUPS = 4
EXPERTS_PER_GROUP = N_EXPERTS // N_GROUPS
TOP_K = 2
SUBLANES = 8
HALO_ROWS = 32
CARRY_ROWS = 8
MOE_WEIGHT_STAGES = 4
DMA_ISSUE_UNROLL = 8
GDN_SOLVE_GROUPS = 2
GDN_KEY_HEADS_PER_STEP = 4
MIB = 1024 * 1024


def _params(semantics, vmem_mib):
    return pltpu.CompilerParams(dimension_semantics=semantics, vmem_limit_bytes=vmem_mib * MIB)


def _sigmoid(x):
    return 0.5 * jnp.tanh(0.5 * x) + 0.5


def _silu(x):
    return x * _sigmoid(x)


def _layer_norm(y, g, b):
    mu = jnp.mean(y, axis=-1, keepdims=True)
    d = y - mu
    var = jnp.mean(d * d, axis=-1, keepdims=True)
    return d * lax.rsqrt(var + LN_EPS) * g + b


def _split_bf16(x):
    hi = x.astype(BF16)
    lo = (x - hi.astype(F32)).astype(BF16)
    return hi, lo


def _dot(a, b):
    return jnp.dot(a, b, preferred_element_type=F32)


def _dot_nt(a, b):
    return lax.dot_general(a, b, (((1,), (1,)), ((), ())), preferred_element_type=F32)


def _dot_tn(a, b):
    return lax.dot_general(a, b, (((0,), (0,)), ((), ())), preferred_element_type=F32)


def _dot3(x, w):
    xh, xl = _split_bf16(x)
    wh, wl = _split_bf16(w)
    return _dot(xh, wh) + _dot(xl, wh) + _dot(xh, wl)


def _pw1_glu_kernel(x_ref, wv_ref, wg_ref, bv_ref, bg_ref, o_ref, wv_s, wg_s):
    @pl.when(pl.program_id(1) == 0)
    def _():
        wv_s[...] = wv_ref[...].astype(BF16)
        wg_s[...] = wg_ref[...].astype(BF16)

    x = x_ref[...]
    val = _dot(x, wv_s[...]) + bv_ref[...]
    gate = _dot(x, wg_s[...]) + bg_ref[...]
    o_ref[...] = val * _sigmoid(gate)


def pw1_glu(xb, w_all, b, layer, *, tm=1024, tn=512):
    m, k = xb.shape
    n = w_all.shape[2] // 2
    tm, tn = min(tm, m), min(tn, n)
    nj = n // tn
    b2 = b.reshape(1, 2 * n)
    w = w_all
    return pl.pallas_call(
        _pw1_glu_kernel,
        grid=(nj, m // tm),
        in_specs=[
            pl.BlockSpec((tm, k), lambda j, i: (i, 0)),
            pl.BlockSpec((None, k, tn), lambda j, i: (layer, 0, j)),
            pl.BlockSpec((None, k, tn), lambda j, i: (layer, 0, j + nj)),
            pl.BlockSpec((1, tn), lambda j, i: (0, j)),
            pl.BlockSpec((1, tn), lambda j, i: (0, j + nj)),
        ],
        out_specs=pl.BlockSpec((tm, tn), lambda j, i: (i, j)),
        out_shape=jax.ShapeDtypeStruct((m, n), F32),
        scratch_shapes=[pltpu.VMEM((k, tn), BF16), pltpu.VMEM((k, tn), BF16)],
        compiler_params=_params(("arbitrary", "arbitrary"), 48),
        name="pw1_glu",
    )(xb, w, w, b2, b2)


def _dwconv_ln_silu_kernel(h_ref, halo_ref, w_ref, b_ref, g_ref, beta_ref, o_ref, buf, cbuf,
                           *, ts, width, cw, rb):
    i = pl.program_id(0)
    d = h_ref.shape[1]
    buf[0:HALO_ROWS, :] = jnp.where(i > 0, halo_ref[...], 0.0)
    buf[HALO_ROWS:HALO_ROWS + ts, :] = h_ref[...]

    def col_body(c, carry):
        cols = pl.ds(pl.multiple_of(c * cw, cw), cw)
        for r in range(ts // rb):
            acc = jnp.broadcast_to(b_ref[:, cols], (rb, cw))
            for b in range(SUBLANES):
                yb = None
                for a in range((width - 1 - b) // SUBLANES + 1):
                    j = width - 1 - (SUBLANES * a + b)
                    start = r * rb + HALO_ROWS - SUBLANES * (a + 1)
                    term = w_ref[j:j + 1, cols] * buf[pl.ds(start, rb + SUBLANES), cols]
                    yb = term if yb is None else yb + term
                acc = acc + yb[SUBLANES - b:SUBLANES - b + rb]
            cbuf[pl.ds(r * rb, rb), cols] = acc
        return carry

    lax.fori_loop(0, d // cw, col_body, 0)
    y = _layer_norm(cbuf[...], g_ref[...], beta_ref[...])
    o_ref[...] = _silu(y).astype(o_ref.dtype)


def dwconv_ln_silu(h, w_dw, b_dw, ln_g, ln_b, *, ts=512, cw=128, rb=128):
    t, d = h.shape
    width = w_dw.shape[0]
    assert SUBLANES * ((width - 1) // SUBLANES + 1) <= HALO_ROWS
    ts, cw = min(ts, t), min(cw, d)
    halo_per_tile = ts // HALO_ROWS
    kern = functools.partial(_dwconv_ln_silu_kernel, ts=ts, width=width, cw=cw, rb=rb)
    row = lambda a: a.reshape(1, d)
    return pl.pallas_call(
        kern,
        grid=(t // ts,),
        in_specs=[
            pl.BlockSpec((ts, d), lambda i: (i, 0)),
            pl.BlockSpec((HALO_ROWS, d), lambda i: (jnp.maximum(i * halo_per_tile - 1, 0), 0)),
            pl.BlockSpec((width, d), lambda i: (0, 0)),
            pl.BlockSpec((1, d), lambda i: (0, 0)),
            pl.BlockSpec((1, d), lambda i: (0, 0)),
            pl.BlockSpec((1, d), lambda i: (0, 0)),
        ],
        out_specs=pl.BlockSpec((ts, d), lambda i: (i, 0)),
        out_shape=jax.ShapeDtypeStruct((t, d), BF16),
        scratch_shapes=[pltpu.VMEM((ts + HALO_ROWS, d), F32), pltpu.VMEM((ts, d), F32)],
        compiler_params=_params(("arbitrary",), 40),
        name="dwconv_ln_silu",
    )(h, h, w_dw, row(b_dw), row(ln_g), row(ln_b))


def _mm_res_ln_kernel(a_ref, w_hbm, bias_ref, res_ref, g_ref, b_ref, o_ref, ob_ref, wres, stage,
                      sem, *, layer, tk, sub):
    nk = wres.shape[0] // tk

    @pl.when(pl.program_id(0) == 0)
    def _():
        def chunk_copy(c):
            return pltpu.make_async_copy(w_hbm.at[layer, pl.ds(c * tk, tk)], stage.at[c % 2],
                                         sem.at[c % 2])
        chunk_copy(0).start()
        for c in range(nk):
            if c + 1 < nk:
                chunk_copy(c + 1).start()
            chunk_copy(c).wait()
            wres[c * tk:(c + 1) * tk, :] = stage[c % 2].astype(BF16)

    for sb in range(a_ref.shape[0] // sub):
        rows = slice(sb * sub, (sb + 1) * sub)
        y = DEEPNORM_ALPHA * res_ref[rows, :] + (_dot(a_ref[rows, :], wres[...]) + bias_ref[...])
        out = _layer_norm(y, g_ref[...], b_ref[...])
        o_ref[rows, :] = out
        ob_ref[rows, :] = out.astype(BF16)


def mm_res_ln(a, w_all, layer, bias, res, ln_g, ln_b, *, tm=256, tk=512, sub=128):
    m, k = a.shape
    n = w_all.shape[2]
    tm, tk = min(tm, m), min(tk, k)
    row = lambda v: v.reshape(1, n)
    return pl.pallas_call(
        functools.partial(_mm_res_ln_kernel, layer=layer, tk=tk, sub=min(sub, tm)),
        grid=(m // tm,),
        in_specs=[
            pl.BlockSpec((tm, k), lambda i: (i, 0)),
            pl.BlockSpec(memory_space=pl.ANY),
            pl.BlockSpec((1, n), lambda i: (0, 0)),
            pl.BlockSpec((tm, n), lambda i: (i, 0)),
            pl.BlockSpec((1, n), lambda i: (0, 0)),
            pl.BlockSpec((1, n), lambda i: (0, 0)),
        ],
        out_specs=[pl.BlockSpec((tm, n), lambda i: (i, 0)),
                   pl.BlockSpec((tm, n), lambda i: (i, 0))],
        out_shape=[jax.ShapeDtypeStruct((m, n), F32), jax.ShapeDtypeStruct((m, n), BF16)],
        scratch_shapes=[pltpu.VMEM((k, n), BF16), pltpu.VMEM((2, tk, n), F32),
                        pltpu.SemaphoreType.DMA((2,))],
        compiler_params=_params(("arbitrary",), 48),
        name="mm_res_ln",
    )(a, w_all, row(bias), res, row(ln_g), row(ln_b))


def _gdn_in_kernel(x_ref, w_ref, wc_ref, o_ref, wbf, pbuf, *, tm, tn, sub, nq, nqk, nconv, cwidth):
    j = pl.program_id(0)
    i = pl.program_id(1)
    heads = [slice(hh * HEAD_DIM, (hh + 1) * HEAD_DIM) for hh in range(tn // HEAD_DIM)]

    @pl.when(i == 0)
    def _():
        wbf[...] = w_ref[...].astype(BF16)
        pbuf[0:CARRY_ROWS, :] = jnp.zeros((CARRY_ROWS, tn), F32)

    def for_sub_blocks(epilogue):
        for sb in range(tm // sub):
            rows = slice(sb * sub, (sb + 1) * sub)
            epilogue(sb, rows, _dot_nt(x_ref[rows, :], wbf[...]))

    def conv_silu(sb, p):
        base = CARRY_ROWS + sb * sub
        pbuf[base:base + sub, :] = p
        c = None
        for jj in range(cwidth):
            term = wc_ref[jj:jj + 1, :] * pbuf[pl.ds(base - (cwidth - 1) + jj, sub), :]
            c = term if c is None else c + term
        return _silu(c)

    def store_heads(rows, s):
        for hh, cols in enumerate(heads):
            o_ref[hh, rows, :] = s[:, cols]

    def gate_epilogue(sb, rows, p):
        store_heads(rows, _silu(p))

    def value_epilogue(sb, rows, p):
        store_heads(rows, conv_silu(sb, p))

    def qk_epilogue(sb, rows, p):
        s = conv_silu(sb, p)
        scale = jnp.where(j < nq, HEAD_DIM ** -0.5, 1.0).astype(F32)
        for hh, cols in enumerate(heads):
            blk = s[:, cols]
            ss = jnp.sum(blk * blk, axis=-1, keepdims=True)
            o_ref[hh, rows, :] = blk * lax.rsqrt(ss + NORM_EPS) * scale

    @pl.when(j >= nconv)
    def _():
        for_sub_blocks(gate_epilogue)

    @pl.when((j >= nqk) & (j < nconv))
    def _():
        for_sub_blocks(value_epilogue)
        pbuf[0:CARRY_ROWS, :] = pbuf[tm:tm + CARRY_ROWS, :]

    @pl.when(j < nqk)
    def _():
        for_sub_blocks(qk_epilogue)
        pbuf[0:CARRY_ROWS, :] = pbuf[tm:tm + CARRY_ROWS, :]


def gdn_in_proj(xb, w_t_all, layer, w_conv, *, tm=2048, tn=512, sub=256):
    t, d = xb.shape
    qk_dim = d
    v_dim = 2 * d
    n_main = 2 * qk_dim + 2 * v_dim
    tm, tn = min(tm, t), min(tn, qk_dim)
    cwidth = w_conv.shape[0]
    nq = qk_dim // tn
    nconv = (2 * qk_dim + v_dim) // tn
    hpt = tn // HEAD_DIM
    kern = functools.partial(_gdn_in_kernel, tm=tm, tn=tn, sub=min(sub, tm), nq=nq, nqk=2 * nq,
                             nconv=nconv, cwidth=cwidth)
    return pl.pallas_call(
        kern,
        grid=(n_main // tn, t // tm),
        in_specs=[
            pl.BlockSpec((tm, d), lambda j, i: (i, 0)),
            pl.BlockSpec((None, tn, d), lambda j, i: (layer, j, 0)),
            pl.BlockSpec((cwidth, tn), lambda j, i: (0, jnp.minimum(j, nconv - 1))),
        ],
        out_specs=pl.BlockSpec((hpt, tm, HEAD_DIM), lambda j, i: (j, i, 0)),
        out_shape=jax.ShapeDtypeStruct((n_main // HEAD_DIM, t, HEAD_DIM), F32),
        scratch_shapes=[pltpu.VMEM((tn, d), BF16), pltpu.VMEM((tm + CARRY_ROWS, tn), F32)],
        compiler_params=_params(("arbitrary", "arbitrary"), 48),
        name="gdn_in_proj",
    )(xb, w_t_all, w_conv)


def _gdn_gate_kernel(x_ref, wt_ref, alog_ref, dtb_ref, col_ref, gct_ref, *, tm, hg):
    nh = alog_ref.shape[1]
    xh, xl = _split_bf16(x_ref[...])
    wh, wl = _split_bf16(wt_ref[...])
    proj = _dot_nt(xh, wh) + _dot_nt(xl, wh) + _dot_nt(xh, wl)
    beta = _sigmoid(proj[:, :nh])
    a = proj[:, nh:] + dtb_ref[...]
    softplus = jnp.maximum(a, 0.0) + jnp.log(1.0 + jnp.exp(-jnp.abs(a)))
    g = -jnp.exp(alog_ref[...]) * softplus
    r = lax.broadcasted_iota(jnp.int32, (tm, tm), 0)
    c = lax.broadcasted_iota(jnp.int32, (tm, tm), 1)
    same = r // GDN_CHUNK == c // GDN_CHUNK
    lower = jnp.where(same & (c <= r), 1.0, 0.0).astype(BF16)
    upper = jnp.where(same & (r <= c), 1.0, 0.0).astype(BF16)
    g1 = g.astype(BF16)
    r1 = g - g1.astype(F32)
    g2 = r1.astype(BF16)
    g3 = (r1 - g2.astype(F32)).astype(BF16)
    gc = _dot(lower, g1) + _dot(lower, g2) + _dot(lower, g3)
    gct_ref[...] = _dot_tn(g1, upper) + _dot_tn(g2, upper) + _dot_tn(g3, upper)
    for grp in range(col_ref.shape[0]):
        cols = slice(grp * hg, (grp + 1) * hg)
        col_ref[grp] = jnp.concatenate([beta[:, cols], gc[:, cols]], axis=1)


def gdn_gates(x, w_t_all, layer, row0, a_log, dt_bias, *, hg, tm=512):
    t, d = x.shape
    h = a_log.shape[0]
    tm = min(tm, t)
    row = lambda v: v.reshape(1, h)
    return pl.pallas_call(
        functools.partial(_gdn_gate_kernel, tm=tm, hg=hg),
        grid=(t // tm,),
        in_specs=[
            pl.BlockSpec((tm, d), lambda i: (i, 0)),
            pl.BlockSpec((None, 2 * h, d), lambda i: (layer, row0 // (2 * h), 0)),
            pl.BlockSpec((1, h), lambda i: (0, 0)),
            pl.BlockSpec((1, h), lambda i: (0, 0)),
        ],
        out_specs=[pl.BlockSpec((h // hg, tm, 2 * hg), lambda i: (0, i, 0)),
                   pl.BlockSpec((h, tm), lambda i: (0, i))],
        out_shape=[jax.ShapeDtypeStruct((h // hg, t, 2 * hg), F32),
                   jax.ShapeDtypeStruct((h, t), F32)],
        compiler_params=_params(("arbitrary",), 32),
        name="gdn_gates",
    )(x, w_t_all, row(a_log), row(dt_bias))


def _gdn_chunk_kernel(q_ref, k_ref, v_ref, z_ref, col_ref, row_ref, nw_ref, o_ref, state,
                      *, nc, kg):
    c_ = GDN_CHUNK
    hd = HEAD_DIM
    vh = 2 * kg

    @pl.when(pl.program_id(1) == 0)
    def _():
        state[...] = jnp.zeros(state.shape, F32)

    ii = lax.broadcasted_iota(jnp.int32, (c_, c_), 0)
    jj = lax.broadcasted_iota(jnp.int32, (c_, c_), 1)
    tril = ii >= jj
    strict = ii > jj

    q = q_ref[...].reshape(kg * nc, c_, hd)
    k = k_ref[...].reshape(kg * nc, c_, hd)
    kq = lax.dot_general(jnp.concatenate([k, q], axis=1).astype(BF16), k.astype(BF16),
                         (((2,), (2,)), ((0,), (0,))), preferred_element_type=F32)
    col = col_ref[...]
    n_steps = c_.bit_length() - 1
    bmm = lambda a, b: jnp.einsum("bij,bjk->bik", a, b, preferred_element_type=F32)
    row2 = lax.broadcasted_iota(jnp.int32, (c_, 2 * c_), 0)
    lane2 = lax.broadcasted_iota(jnp.int32, (c_, 2 * c_), 1)
    lane_lo = lane2 < c_
    eye2 = jnp.where((row2 == lane2) | (row2 + c_ == lane2), 1.0, 0.0)

    def block_diag(packed):
        return jnp.concatenate([jnp.where(lane_lo, packed, 0.0), jnp.where(lane_lo, 0.0, packed)],
                               axis=1).astype(BF16)

    def solve_heads(heads):
        neg_a, rhs, qk, k_dec, q_dec, g_last = [], [], [], [], [], []
        for h in heads:
            own = slice((h // 2) * nc, (h // 2 + 1) * nc)
            bcol = jnp.broadcast_to(col[:, h:h + 1], (nc * c_, hd)).reshape(nc, c_, hd)
            gcol = jnp.broadcast_to(col[:, vh + h:vh + h + 1], (nc * c_, hd)).reshape(nc, c_, hd)
            grow = row_ref[h]
            decay = jnp.where(tril, jnp.exp(jnp.minimum(gcol[:, :, :c_] - grow, 0.0)), 0.0)
            neg_a.append(jnp.where(strict, -(bcol[:, :, :c_] * kq[own, :c_] * decay), 0.0))
            qk.append((kq[own, c_:] * decay).astype(BF16))
            eg = jnp.exp(gcol)
            glast = gcol[:, c_ - 1:c_, :]
            rhs.append(jnp.concatenate([v_ref[h].reshape(nc, c_, hd) * bcol, k[own] * (bcol * eg)],
                                       axis=2))
            k_dec.append((k[own] * jnp.exp(glast - gcol)).astype(BF16))
            q_dec.append(q[own] * eg)
            g_last.append(jnp.exp(glast))

        def chunk_major(per_head):
            stacked = jnp.stack(per_head, axis=1)
            return stacked.reshape((nc * len(heads),) + stacked.shape[2:])

        pw = chunk_major(neg_a)
        rhs_all = chunk_major(rhs)
        half = (nc * len(heads)) // 2
        q_pow = jnp.concatenate([pw[:half], pw[half:]], axis=2)
        t_inv = eye2 + q_pow
        for s in range(1, n_steps):
            q_pow = bmm(q_pow.astype(BF16), block_diag(q_pow))
            t_inv = t_inv + bmm(q_pow.astype(BF16), block_diag(t_inv))
        stacked_rhs = jnp.concatenate([rhs_all[:half], rhs_all[half:]], axis=1)
        corr = bmm(block_diag(t_inv - eye2), stacked_rhs.astype(BF16))
        sol = rhs_all + jnp.concatenate([corr[:, :c_], corr[:, c_:]], axis=0)
        wq = jnp.concatenate([sol[:, :, hd:], chunk_major(q_dec)], axis=1).astype(BF16)
        return sol[:, :, :hd], wq, chunk_major(qk), chunk_major(k_dec), chunk_major(g_last)

    gh = vh // GDN_SOLVE_GROUPS
    solved = [solve_heads(range(g * gh, (g + 1) * gh)) for g in range(GDN_SOLVE_GROUPS)]

    def per_chain(field, c, h):
        return solved[h // gh][field][c * gh + h % gh]

    nw = nw_ref[...]
    s_cur = [state[h] for h in range(vh)]
    for c in range(nc):
        rows = slice(c * c_, (c + 1) * c_)
        ws = [_dot(per_chain(1, c, h), s_cur[h].astype(BF16)) for h in range(vh)]
        v_new = [(per_chain(0, c, h) - ws[h][:c_]).astype(BF16) for h in range(vh)]
        o = [ws[h][c_:] + _dot(per_chain(2, c, h), v_new[h]) for h in range(vh)]
        s_cur = [s_cur[h] * per_chain(4, c, h) + _dot_tn(per_chain(3, c, h), v_new[h])
                 for h in range(vh)]
        for h in range(vh):
            on = o[h] * lax.rsqrt(jnp.mean(o[h] * o[h], axis=-1, keepdims=True) + NORM_EPS) * nw
            o_ref[rows, h * hd:(h + 1) * hd] = (on * z_ref[h, rows, :]).astype(o_ref.dtype)
    for h in range(vh):
        state[h] = s_cur[h]


def gdn_chunk(proj, col, gct, norm_w, *, kg, tb=512):
    t = proj.shape[1]
    nkh = proj.shape[0] // 6
    vh = 2 * kg
    tb = min(tb, t)
    nc = tb // GDN_CHUNK
    n_chunks = t // GDN_CHUNK
    row = gct.reshape(nkh // kg, vh, n_chunks, 1, GDN_CHUNK)
    return pl.pallas_call(
        functools.partial(_gdn_chunk_kernel, nc=nc, kg=kg),
        grid=(nkh // kg, t // tb),
        in_specs=[
            pl.BlockSpec((kg, tb, HEAD_DIM), lambda g, n: (g, n, 0)),
            pl.BlockSpec((kg, tb, HEAD_DIM), lambda g, n: (nkh // kg + g, n, 0)),
            pl.BlockSpec((vh, tb, HEAD_DIM), lambda g, n: (2 * nkh // vh + g, n, 0)),
            pl.BlockSpec((vh, tb, HEAD_DIM), lambda g, n: (4 * nkh // vh + g, n, 0)),
            pl.BlockSpec((None, tb, 2 * vh), lambda g, n: (g, n, 0)),
            pl.BlockSpec((None, vh, nc, 1, GDN_CHUNK), lambda g, n: (g, 0, n, 0, 0)),
            pl.BlockSpec((1, HEAD_DIM), lambda g, n: (0, 0)),
        ],
        out_specs=pl.BlockSpec((tb, vh * HEAD_DIM), lambda g, n: (n, g)),
        out_shape=jax.ShapeDtypeStruct((t, 2 * nkh * HEAD_DIM), BF16),
        scratch_shapes=[pltpu.VMEM((vh, HEAD_DIM, HEAD_DIM), F32)],
        compiler_params=_params(("arbitrary", "arbitrary"), 48),
        name="gdn_chunk",
    )(proj, proj, proj, proj, col, row, norm_w.reshape(1, HEAD_DIM))


def _top2_of_4(a, b, c, d):
    hi1, lo1 = jnp.maximum(a, b), jnp.minimum(a, b)
    hi2, lo2 = jnp.maximum(c, d), jnp.minimum(c, d)
    return jnp.maximum(hi1, hi2), jnp.maximum(jnp.minimum(hi1, hi2), jnp.maximum(lo1, lo2))


def _first_argmax(vals):
    best, arg = vals[0], jnp.zeros(vals[0].shape, jnp.int32)
    for n in range(1, len(vals)):
        upd = vals[n] > best
        arg = jnp.where(upd, n, arg)
        best = jnp.where(upd, vals[n], best)
    return best, arg


def _router_kernel(x_ref, w_ref, b_ref, idx_ref, gate_ref, rank_ref, cnt_ref, run, *, tm):
    @pl.when(pl.program_id(0) == 0)
    def _():
        run[...] = jnp.zeros(run.shape, F32)

    xh, xl = _split_bf16(x_ref[...])
    wh, wl = _split_bf16(w_ref[...])
    logits = _dot_nt(wh, xh) + _dot_nt(wh, xl) + _dot_nt(wl, xh) + b_ref[...]
    e = jnp.exp(logits - jnp.max(logits, axis=0, keepdims=True))
    p = e / jnp.sum(e, axis=0, keepdims=True)
    rows = [p[r:r + 1, :] for r in range(N_EXPERTS)]
    scores = []
    for g in range(N_GROUPS):
        t1, t2 = _top2_of_4(*rows[g * EXPERTS_PER_GROUP:(g + 1) * EXPERTS_PER_GROUP])
        scores.append(t1 + t2)
    _, grp = _first_argmax(scores)
    vals = []
    for j in range(EXPERTS_PER_GROUP):
        v = rows[(N_GROUPS - 1) * EXPERTS_PER_GROUP + j]
        for g in range(N_GROUPS - 2, -1, -1):
            v = jnp.where(grp == g, rows[g * EXPERTS_PER_GROUP + j], v)
        vals.append(v)
    v1, i1 = _first_argmax(vals)
    v2, i2 = _first_argmax([jnp.where(i1 == j, -1.0, vals[j]) for j in range(EXPERTS_PER_GROUP)])
    e1 = grp * EXPERTS_PER_GROUP + i1
    e2 = grp * EXPERTS_PER_GROUP + i2
    den = v1 + v2
    idx_ref[0:1, :] = e1
    idx_ref[1:2, :] = e2
    gate_ref[0:1, :] = v1 / den
    gate_ref[1:2, :] = v2 / den

    eio = lax.broadcasted_iota(jnp.int32, (N_EXPERTS, tm), 0)
    hit1 = eio == e1
    hit2 = eio == e2
    onehot = jnp.where(hit1 | hit2, 1.0, 0.0)
    r = lax.broadcasted_iota(jnp.int32, (tm, tm), 0)
    c = lax.broadcasted_iota(jnp.int32, (tm, tm), 1)
    before = jnp.where(r < c, 1.0, 0.0).astype(BF16)
    prefix = _dot(onehot.astype(BF16), before) + run[:, 0:1]
    rank_ref[0:1, :] = jnp.sum(jnp.where(hit1, prefix, 0.0), axis=0, keepdims=True).astype(jnp.int32)
    rank_ref[1:2, :] = jnp.sum(jnp.where(hit2, prefix, 0.0), axis=0, keepdims=True).astype(jnp.int32)
    run[...] = run[...] + jnp.sum(onehot, axis=1, keepdims=True)
    cnt_ref[...] = run[...].astype(jnp.int32)


def moe_router(x, router_w, router_b, *, tm=512):
    t, d = x.shape
    tm = min(tm, t)
    out2 = lambda dt: jax.ShapeDtypeStruct((TOP_K, t), dt)
    blk2 = pl.BlockSpec((TOP_K, tm), lambda i: (0, i))
    return pl.pallas_call(
        functools.partial(_router_kernel, tm=tm),
        grid=(t // tm,),
        in_specs=[
            pl.BlockSpec((tm, d), lambda i: (i, 0)),
            pl.BlockSpec((N_EXPERTS, d), lambda i: (0, 0)),
            pl.BlockSpec((N_EXPERTS, 1), lambda i: (0, 0)),
        ],
        out_specs=[blk2, blk2, blk2, pl.BlockSpec((N_EXPERTS, 128), lambda i: (0, 0))],
        out_shape=[out2(jnp.int32), out2(F32), out2(jnp.int32),
                   jax.ShapeDtypeStruct((N_EXPERTS, 128), jnp.int32)],
        scratch_shapes=[pltpu.VMEM((N_EXPERTS, 128), F32)],
        compiler_params=_params(("arbitrary",), 32),
        name="moe_router",
    )(x, router_w.T, router_b.reshape(N_EXPERTS, 1))


def _dispatch_kernel(dest_ref, pend_ref, x_ref, xs_hbm, zbuf, sem, zsem, *, td, t, tm):
    base = pl.program_id(0) * td

    def zero_tile_copy(start):
        return pltpu.make_async_copy(zbuf, xs_hbm.at[pl.ds(pl.multiple_of(start, tm), tm)], zsem)

    @pl.when(pl.program_id(0) == 0)
    def _():
        zbuf[...] = jnp.zeros(zbuf.shape, zbuf.dtype)
        n_rows = xs_hbm.shape[0]
        starts, valid = [], []
        for e in range(N_EXPERTS):
            starts.append(pend_ref[e] - tm)
            valid.append(pend_ref[e] > (pend_ref[e - 1] if e else 0))
            starts.append(pend_ref[N_EXPERTS - 1] + e * tm)
            valid.append(starts[-1] < n_rows)
        for start, ok in zip(starts, valid):
            @pl.when(ok)
            def _():
                zero_tile_copy(start).start()
        for start, ok in zip(starts, valid):
            @pl.when(ok)
            def _():
                zero_tile_copy(start).wait()

    def row_copy(src_row, dst_row):
        return pltpu.make_async_copy(x_ref.at[pl.ds(src_row, 1)], xs_hbm.at[pl.ds(dst_row, 1)], sem)

    def issue(n, carry):
        for k in range(TOP_K):
            row_copy(n, dest_ref[k * t + base + n]).start()
        return carry

    lax.fori_loop(0, td, issue, 0, unroll=DMA_ISSUE_UNROLL)
    all_rows = xs_hbm.at[pl.ds(0, TOP_K * td)]
    pltpu.make_async_copy(all_rows, all_rows, sem).wait()


def moe_dispatch(x, dest_flat, pend, n_rows, *, tm, td=512):
    t, d = x.shape
    td = min(td, t)
    return pl.pallas_call(
        functools.partial(_dispatch_kernel, td=td, t=t, tm=tm),
        grid_spec=pltpu.PrefetchScalarGridSpec(
            num_scalar_prefetch=2,
            grid=(t // td,),
            in_specs=[pl.BlockSpec((td, d), lambda i, dest, pend: (i, 0))],
            out_specs=pl.BlockSpec(memory_space=pl.ANY),
            scratch_shapes=[pltpu.VMEM((tm, d), x.dtype), pltpu.SemaphoreType.DMA(()),
                            pltpu.SemaphoreType.DMA(())],
        ),
        out_shape=jax.ShapeDtypeStruct((n_rows, d), x.dtype),
        compiler_params=_params(("arbitrary",), 32),
        name="moe_dispatch",
    )(dest_flat, pend, x)


def _moe_ffn_kernel(te_ref, first_ref, next_ref, nu_ref, x_ref, wg_hbm, wu_hbm, wd_hbm, y_ref,
                    wg_res, wu_res, wd_res, wg_st, wu_st, wd_st, sem, *, layer, tf):
    i = pl.program_id(0)
    n_chunks = wg_res.shape[1] // tf
    n_stage = wg_st.shape[0]
    chunk = lambda c: slice(c * tf, (c + 1) * tf)

    def chunk_copies(expert, c):
        slot = c % n_stage
        return (
            pltpu.make_async_copy(wg_hbm.at[layer, expert, :, chunk(c)], wg_st.at[slot], sem.at[0, slot]),
            pltpu.make_async_copy(wu_hbm.at[layer, expert, :, chunk(c)], wu_st.at[slot], sem.at[1, slot]),
            pltpu.make_async_copy(wd_hbm.at[layer, expert, chunk(c), :], wd_st.at[slot], sem.at[2, slot]),
        )

    def start_chunk(expert, c):
        for copy in chunk_copies(expert, c):
            copy.start()

    def land_chunk(expert, c):
        for copy in chunk_copies(expert, c):
            copy.wait()
        wg_res[:, chunk(c)] = wg_st[c % n_stage].astype(BF16)
        wu_res[:, chunk(c)] = wu_st[c % n_stage].astype(BF16)
        wd_res[chunk(c), :] = wd_st[c % n_stage].astype(BF16)

    def ffn_chunk(x, c):
        h = _silu(_dot(x, wg_res[:, chunk(c)])) * _dot(x, wu_res[:, chunk(c)])
        return _dot(h.astype(BF16), wd_res[chunk(c), :])

    active = i < nu_ref[0]
    is_first = first_ref[i] == 1

    @pl.when(jnp.logical_not(active))
    def _():
        y_ref[...] = jnp.zeros(y_ref.shape, F32)

    @pl.when(i == 0)
    def _():
        for c in range(min(n_stage, n_chunks)):
            start_chunk(te_ref[0], c)

    @pl.when(active & is_first)
    def _():
        expert = te_ref[i]
        x = x_ref[...].astype(BF16)
        acc = None
        for c in range(n_chunks):
            land_chunk(expert, c)
            if c + n_stage < n_chunks:
                start_chunk(expert, c + n_stage)
            part = ffn_chunk(x, c)
            acc = part if acc is None else acc + part
        y_ref[...] = acc

        @pl.when(next_ref[i] >= 0)
        def _():
            for c in range(min(n_stage, n_chunks)):
                start_chunk(next_ref[i], c)

    @pl.when(active & jnp.logical_not(is_first))
    def _():
        x = x_ref[...].astype(BF16)
        acc = None
        for c in range(n_chunks):
            part = ffn_chunk(x, c)
            acc = part if acc is None else acc + part
        y_ref[...] = acc


def moe_ffn(xs, tile_expert, tile_first, tile_next, n_used, wg, wu, wd, layer, *, tm, tf=256):
    n_rows, d = xs.shape
    f_dim = wg.shape[3]
    tf = math.gcd(tf, f_dim)
    row_map = lambda i, te, first, nxt, nu: (jnp.minimum(i, nu[0] - 1), 0)
    any_spec = pl.BlockSpec(memory_space=pl.ANY)
    return pl.pallas_call(
        functools.partial(_moe_ffn_kernel, layer=layer, tf=tf),
        grid_spec=pltpu.PrefetchScalarGridSpec(
            num_scalar_prefetch=4,
            grid=(n_rows // tm,),
            in_specs=[pl.BlockSpec((tm, d), row_map), any_spec, any_spec, any_spec],
            out_specs=pl.BlockSpec((tm, d), lambda i, te, first, nxt, nu: (i, 0)),
            scratch_shapes=[
                pltpu.VMEM((d, f_dim), BF16), pltpu.VMEM((d, f_dim), BF16), pltpu.VMEM((f_dim, d), BF16),
                pltpu.VMEM((MOE_WEIGHT_STAGES, d, tf), F32), pltpu.VMEM((MOE_WEIGHT_STAGES, d, tf), F32),
                pltpu.VMEM((MOE_WEIGHT_STAGES, tf, d), F32),
                pltpu.SemaphoreType.DMA((3, MOE_WEIGHT_STAGES)),
            ],
        ),
        out_shape=jax.ShapeDtypeStruct((n_rows, d), F32),
        compiler_params=_params(("arbitrary",), 60),
        name="moe_ffn",
    )(tile_expert, tile_first, tile_next, n_used, xs, wg, wu, wd)


def _combine_kernel(dest_ref, y_hbm, x_ref, gt_ref, g_ref, b_ref, o_ref, ob_ref, buf, sem,
                    *, tc, n_tiles, t):
    i = pl.program_id(0)

    def row_copy(src_row, slot, k, n):
        return pltpu.make_async_copy(y_hbm.at[pl.ds(src_row, 1)], buf.at[slot, k, pl.ds(n, 1)],
                                     sem.at[slot])

    @pl.when(i == 0)
    def _():
        def body(n, carry):
            for k in range(TOP_K):
                row_copy(dest_ref[k * t + n], 0, k, n).start()
            return carry
        lax.fori_loop(0, tc, body, 0, unroll=DMA_ISSUE_UNROLL)

    def wait_rows(slot):
        pltpu.make_async_copy(buf.at[slot], buf.at[slot], sem.at[slot]).wait()

    def finish(slot):
        gt = gt_ref[...]
        mix = gt[:, 0:1] * buf[slot, 0] + gt[:, 1:2] * buf[slot, 1]
        out = _layer_norm(DEEPNORM_ALPHA * x_ref[...] + mix, g_ref[...], b_ref[...])
        o_ref[...] = out
        ob_ref[...] = out.astype(BF16)

    def gather_next_and_finish(slot):
        wait_rows(slot)
        base = (i + 1) * tc
        for n in range(tc):
            for k in range(TOP_K):
                row_copy(dest_ref[k * t + base + n], 1 - slot, k, n).start()
        finish(slot)

    has_next = i + 1 < n_tiles

    @pl.when(has_next & (i % 2 == 0))
    def _():
        gather_next_and_finish(0)

    @pl.when(has_next & (i % 2 == 1))
    def _():
        gather_next_and_finish(1)

    @pl.when(jnp.logical_not(has_next))
    def _():
        wait_rows((n_tiles - 1) % 2)
        finish((n_tiles - 1) % 2)


def moe_combine(y, dest_flat, x, gates_t, ln_g, ln_b, *, tc=256):
    t, d = x.shape
    tc = min(tc, t)
    n_tiles = t // tc
    row = lambda v: v.reshape(1, d)
    blk = lambda i, dest: (i, 0)
    fixed = lambda i, dest: (0, 0)
    return pl.pallas_call(
        functools.partial(_combine_kernel, tc=tc, n_tiles=n_tiles, t=t),
        grid_spec=pltpu.PrefetchScalarGridSpec(
            num_scalar_prefetch=1,
            grid=(n_tiles,),
            in_specs=[
                pl.BlockSpec(memory_space=pl.ANY),
                pl.BlockSpec((tc, d), blk),
                pl.BlockSpec((tc, TOP_K), blk),
                pl.BlockSpec((1, d), fixed),
                pl.BlockSpec((1, d), fixed),
            ],
            out_specs=[pl.BlockSpec((tc, d), blk), pl.BlockSpec((tc, d), blk)],
            scratch_shapes=[pltpu.VMEM((2, TOP_K, tc, d), F32), pltpu.SemaphoreType.DMA((2,))],
        ),
        out_shape=[jax.ShapeDtypeStruct((t, d), F32), jax.ShapeDtypeStruct((t, d), BF16)],
        compiler_params=_params(("arbitrary",), 40),
        name="moe_combine",
    )(dest_flat, y, x, gates_t, row(ln_g), row(ln_b))


def moe_layer(x, router_w, router_b, wg, wu, wd, layer, ln_g, ln_b, *, tm=256):
    t, d = x.shape
    idx, gates, rank, cnt = moe_router(x, router_w, router_b)
    counts = cnt[:, 0]
    padded = (counts + tm - 1) // tm * tm
    pend = jnp.cumsum(padded).astype(jnp.int32)
    pstart = pend - padded
    experts = jnp.arange(N_EXPERTS, dtype=jnp.int32)
    seg_start = jnp.sum(jnp.where(idx[:, :, None] == experts, pstart, 0), axis=-1)
    dest = (seg_start + rank).reshape(TOP_K * t).astype(jnp.int32)
    n_tiles = (t * TOP_K) // tm + N_EXPERTS
    n_used = pend[-1] // tm
    tile_start = jnp.arange(n_tiles, dtype=jnp.int32) * tm
    te = jnp.sum((pend[None, :] <= tile_start[:, None]).astype(jnp.int32), axis=1)
    last_used = jnp.sum((pend <= (n_used - 1) * tm).astype(jnp.int32))
    te = jnp.where(tile_start < pend[-1], te, last_used).astype(jnp.int32)
    first = jnp.concatenate([jnp.ones((1,), jnp.int32), (te[1:] != te[:-1]).astype(jnp.int32)])
    later_nonempty = (counts[None, :] > 0) & (experts[None, :] > experts[:, None])
    next_expert = jnp.min(jnp.where(later_nonempty, experts[None, :], N_EXPERTS), axis=1)
    next_expert = jnp.where(next_expert < N_EXPERTS, next_expert, -1)
    tile_next = jnp.sum(jnp.where(te[:, None] == experts, next_expert, 0), axis=1).astype(jnp.int32)

    xs = moe_dispatch(x, dest, pend, n_tiles * tm, tm=tm)
    y = moe_ffn(xs, te, first, tile_next, n_used.reshape(1), wg, wu, wd, layer, tm=tm)
    return moe_combine(y, dest, x, gates.T, ln_g, ln_b)


def kernel(x, conv_w_pw1, conv_b_pw1, conv_w_dw, conv_b_dw, conv_ln_g, conv_ln_b, conv_w_pw2,
           conv_b_pw2, gdn_w_in, gdn_w_conv, gdn_a_log, gdn_dt_bias, gdn_norm_w, gdn_w_out,
           router_w, router_b, moe_w_gate, moe_w_up, moe_w_down, ln_mix_g, ln_mix_b, ln_ffn_g,
           ln_ffn_b):
    b_, s_, d = x.shape
    t = b_ * s_
    xf = x.reshape(t, d)
    xb = xf.astype(BF16)
    n_main = 6 * d
    n_vh = gdn_a_log.shape[1]
    kg = min(GDN_KEY_HEADS_PER_STEP, n_vh // 2)
    w_in_t = jnp.swapaxes(gdn_w_in, 1, 2)
    for i in range(DEPTH):
        j = i // 2
        if i % 2 == 0:
            h = pw1_glu(xb, conv_w_pw1, conv_b_pw1[j], j)
            hb = dwconv_ln_silu(h, conv_w_dw[j], conv_b_dw[j], conv_ln_g[j], conv_ln_b[j])
            xf, xb = mm_res_ln(hb, conv_w_pw2, j, conv_b_pw2[j], xf, ln_mix_g[i], ln_mix_b[i])
        else:
            proj = gdn_in_proj(xb, w_in_t, j, gdn_w_conv[j])
            col, gct = gdn_gates(xf, w_in_t, j, n_main, gdn_a_log[j], gdn_dt_bias[j], hg=2 * kg)
            ob = gdn_chunk(proj, col, gct, gdn_norm_w[j], kg=kg)
            xf, xb = mm_res_ln(ob, gdn_w_out, j, jnp.zeros((d,), F32), xf, ln_mix_g[i], ln_mix_b[i])
        xf, xb = moe_layer(xf, router_w, router_b, moe_w_gate, moe_w_up, moe_w_down, i,
                           ln_ffn_g[i], ln_ffn_b[i])
    return xf.reshape(b_, s_, d)
```
